```python
import jax, jax.numpy as jnp
from jax import lax
import numpy as np


D_MODEL = 1024
BATCH = 4
SEQ = 4096
DEPTH = 2
DEC_BATCH = 8
DEC_SEQ = 64
PAST_LEN = 1024

CHUNK = 64
Q_BLOCK = 128
MIX_WIDTH = D_MODEL
GLA_WIDTH = MIX_WIDTH // 2
GLA_HEADS = 4
GLA_DV = GLA_WIDTH // GLA_HEADS
GLA_DK = GLA_DV // 2
GLA_KEY = GLA_HEADS * GLA_DK
GATE_RANK = 16
GATE_NORM = 16.0
SB_WIDTH = MIX_WIDTH - GLA_WIDTH
SB_HEADS = 8
SB_HD = SB_WIDTH // SB_HEADS
D_FF = 4 * D_MODEL
EPS = 1e-6
SPLITS = (GLA_KEY, GLA_KEY, GLA_WIDTH, GLA_WIDTH, GATE_RANK, SB_WIDTH, SB_WIDTH, SB_WIDTH)
IN_COLS = sum(SPLITS)
SPLIT_IDX = [int(i) for i in np.cumsum(SPLITS)[:-1]]

kernel_name = 'gla_stickbreaking_hybrid_stream_step'


def rmsnorm(x, g):
    xf = x.astype(jnp.float32)
    return xf * lax.rsqrt(jnp.mean(xf * xf, axis=-1, keepdims=True) + EPS) * g.astype(jnp.float32)


def project(x, norm_g, w_in, w_a2, b_a2, q_g, k_g):
    B, T = x.shape[0], x.shape[1]
    xn = rmsnorm(x, norm_g).astype(x.dtype)
    proj = (xn @ w_in).astype(jnp.float32)
    gq, gk, gv, gate, alr, sq, sk, sv = jnp.split(proj, SPLIT_IDX, axis=-1)
    log_a = jax.nn.log_sigmoid(alr @ w_a2.astype(jnp.float32) + b_a2.astype(jnp.float32)) / GATE_NORM

    def heads(t, h):
        return t.reshape(B, T, h, -1).transpose(0, 2, 1, 3)

    gla = (heads(gq, GLA_HEADS) * GLA_DK ** -0.5, heads(gk, GLA_HEADS), heads(gv, GLA_HEADS), heads(log_a, GLA_HEADS))
    sb = (rmsnorm(sq.reshape(B, T, SB_HEADS, SB_HD), q_g),
          rmsnorm(sk.reshape(B, T, SB_HEADS, SB_HD), k_g),
          sv.reshape(B, T, SB_HEADS, SB_HD))
    return gla, sb, gate


def gla_chunk(S, inp):
    q, k, v, la = inp
    L = q.shape[2]
    b = jnp.cumsum(la, axis=2)
    causal = jnp.tril(jnp.ones((L, L), dtype=bool))
    diff = b[:, :, :, None, :] - b[:, :, None, :, :]
    decay = jnp.exp(jnp.where(causal[None, None, :, :, None], diff, -jnp.inf))
    scores = jnp.einsum('bhtd,bhsd,bhtsd->bhts', q, k, decay)
    o = jnp.einsum('bhts,bhsv->bhtv', scores, v) + jnp.einsum('bhtd,bhdv->bhtv', q * jnp.exp(b), S)
    b_end = b[:, :, -1:, :]
    S_new = jnp.exp(b_end[:, :, 0, :])[..., None] * S + jnp.einsum('bhsd,bhsv->bhdv', k * jnp.exp(b_end - b), v)
    return S_new, o


def gla_prompt(q, k, v, la):
    B, H, T, _ = q.shape
    n = T // CHUNK

    def chunks(t):
        return jnp.moveaxis(t.reshape(B, H, n, CHUNK, t.shape[-1]), 2, 0)

    S0 = jnp.zeros((B, H, GLA_DK, GLA_DV), jnp.float32)
    S, o = lax.scan(gla_chunk, S0, (chunks(q), chunks(k), chunks(v), chunks(la)))
    o = jnp.moveaxis(o, 0, 2).reshape(B, H, T, GLA_DV)
    return S, o


def sb_block(q, q_pos, k, v, k_pos):
    z = jnp.einsum('bqhd,bkhd->bhqk', q, k) * SB_HD ** -0.5
    mask = (k_pos[None, :] < q_pos[:, None])[None, None]
    log_beta = jax.nn.log_sigmoid(z)
    l = jnp.where(mask, jax.nn.log_sigmoid(-z), 0.0)
    c = lax.cumsum(l, axis=3, reverse=True) - l
    w = jnp.where(mask, jnp.exp(log_beta + c), 0.0)
    return jnp.einsum('bhqk,bkhd->bqhd', w, v)


def sb_prompt(q, k, v):
    B, T, H, d = q.shape
    nb = T // Q_BLOCK
    pos = jnp.arange(T)
    qb = jnp.moveaxis(q.reshape(B, nb, Q_BLOCK, H, d), 1, 0)
    starts = jnp.arange(nb) * Q_BLOCK
    out = lax.map(lambda a: sb_block(a[0], a[1] + jnp.arange(Q_BLOCK), k, v, pos), (qb, starts))
    return jnp.moveaxis(out, 0, 1).reshape(B, T, H, d)


def finish(x, o_gla, gate, o_sb, gla_norm_g, w_out, norm2_g, w_up, w_down):
    B, T = x.shape[0], x.shape[1]
    og = rmsnorm(o_gla, gla_norm_g).transpose(0, 2, 1, 3).reshape(B, T, GLA_WIDTH) * jax.nn.silu(gate)
    mix = jnp.concatenate([og, o_sb.reshape(B, T, SB_WIDTH)], axis=-1).astype(x.dtype)
    h = x + mix @ w_out
    u = rmsnorm(h, norm2_g).astype(x.dtype) @ w_up
    return h + jnp.square(jax.nn.relu(u)) @ w_down


def setup_inputs(seed: int = 0) -> dict:
    key = jax.random.key(seed)
    ks = jax.random.split(key, 20)
    f32 = jnp.float32
    nrm = lambda k, s, sc: jax.random.normal(k, s, f32) * sc
    return {
        'x_prompt': nrm(ks[0], (BATCH, SEQ, D_MODEL), 1.0),
        'x_sample': nrm(ks[1], (DEC_BATCH, DEC_SEQ, D_MODEL), 1.0),
        'cache_sb_k': nrm(ks[2], (DEPTH, DEC_BATCH, PAST_LEN, SB_HEADS, SB_HD), 1.0),
        'cache_sb_v': nrm(ks[3], (DEPTH, DEC_BATCH, PAST_LEN, SB_HEADS, SB_HD), 1.0),
        'state_gla': nrm(ks[4], (DEPTH, DEC_BATCH, GLA_HEADS, GLA_DK, GLA_DV), 0.5),
        'norm1_g': 1.0 + nrm(ks[5], (DEPTH, D_MODEL), 0.02),
        'w_in': nrm(ks[6], (DEPTH, D_MODEL, IN_COLS), D_MODEL ** -0.5),
        'w_a2': nrm(ks[7], (DEPTH, GATE_RANK, GLA_KEY), GATE_RANK ** -0.5),
        'b_a2': nrm(ks[8], (DEPTH, GLA_KEY), 0.1),
        'q_norm_g': 1.0 + nrm(ks[9], (DEPTH, SB_HD), 0.02),
        'k_norm_g': 1.0 + nrm(ks[10], (DEPTH, SB_HD), 0.02),
        'gla_norm_g': 1.0 + nrm(ks[11], (DEPTH, GLA_DV), 0.02),
        'w_out': nrm(ks[12], (DEPTH, MIX_WIDTH, D_MODEL), MIX_WIDTH ** -0.5),
        'norm2_g': 1.0 + nrm(ks[13], (DEPTH, D_MODEL), 0.02),
        'w_up': nrm(ks[14], (DEPTH, D_MODEL, D_FF), D_MODEL ** -0.5),
        'w_down': nrm(ks[15], (DEPTH, D_FF, D_MODEL), D_FF ** -0.5),
    }


def reference(x_prompt, x_sample, cache_sb_k, cache_sb_v, state_gla, norm1_g, w_in, w_a2, b_a2,
              q_norm_g, k_norm_g, gla_norm_g, w_out, norm2_g, w_up, w_down):
    dt = x_prompt.dtype
    xp, xs = x_prompt, x_sample
    kp_l, vp_l, sp_l, ks_l, vs_l, ss_l = [], [], [], [], [], []
    P = cache_sb_k.shape[2]
    for l in range(DEPTH):
        lw = (norm1_g[l], w_in[l], w_a2[l], b_a2[l], q_norm_g[l], k_norm_g[l])
        tail = (gla_norm_g[l], w_out[l], norm2_g[l], w_up[l], w_down[l])
        gla_in, (sq, sk, sv), gate = project(xp, *lw)
        S_p, o_g = gla_prompt(*gla_in)
        o_s = sb_prompt(sq, sk, sv)
        xp = finish(xp, o_g, gate, o_s, *tail)
        kp_l.append(sk.astype(dt))
        vp_l.append(sv.astype(dt))
        sp_l.append(S_p.astype(dt))
        T = xs.shape[1]
        gla_in, (sq, sk, sv), gate = project(xs, *lw)
        S_s, o_g = gla_chunk(state_gla[l].astype(jnp.float32), gla_in)
        k_all = jnp.concatenate([cache_sb_k[l].astype(jnp.float32), sk], axis=1)
        v_all = jnp.concatenate([cache_sb_v[l].astype(jnp.float32), sv], axis=1)
        o_s = sb_block(sq, P + jnp.arange(T), k_all, v_all, jnp.arange(P + T))
        xs = finish(xs, o_g, gate, o_s, *tail)
        ks_l.append(sk.astype(dt))
        vs_l.append(sv.astype(dt))
        ss_l.append(S_s.astype(dt))
    return (xp, xs, jnp.stack(kp_l), jnp.stack(vp_l), jnp.stack(sp_l), jnp.stack(ks_l), jnp.stack(vs_l), jnp.stack(ss_l))
```

```python
import functools

import jax
import jax.numpy as jnp
from jax import lax
from jax.experimental import pallas as pl
from jax.experimental.pallas import tpu as pltpu

D_MODEL = 1024
DEPTH = 2
CHUNK = 64
GLA_HEADS = 4
GLA_DK = 64
GLA_DV = 128
GLA_KEY = GLA_HEADS * GLA_DK
GLA_WIDTH = GLA_HEADS * GLA_DV
GATE_RANK = 16
GATE_NORM = 16.0
SB_HEADS = 8
SB_HD = 64
SB_WIDTH = SB_HEADS * SB_HD
D_FF = 4 * D_MODEL
EPS = 1e-6

LANES = 128
VMEM_LIMIT = 56 * 1024 * 1024

F32 = jnp.float32
BF16 = jnp.bfloat16


def _dot(a, b):
    return jnp.dot(a, b, preferred_element_type=F32)


def _dot_nt(a, b):
    return lax.dot_general(a, b, (((1,), (1,)), ((), ())), preferred_element_type=F32)


def _iota(shape, dim):
    return lax.broadcasted_iota(jnp.int32, shape, dim)


def _split3(x):
    p1 = x.astype(BF16)
    r1 = x - p1.astype(F32)
    p2 = r1.astype(BF16)
    p3 = (r1 - p2.astype(F32)).astype(BF16)
    return p1, p2, p3


def _split2(x):
    p1 = x.astype(BF16)
    p2 = (x - p1.astype(F32)).astype(BF16)
    return p1, p2


def _softplus(z):
    return jnp.maximum(z, 0.0) + jnp.log(1.0 + jnp.exp(-jnp.abs(z)))


def _rmsnorm_rows(x, g):
    ms = jnp.mean(x * x, axis=-1, keepdims=True)
    return x * lax.rsqrt(ms + EPS) * g


def _project_kernel(x_ref, g1_ref, wqk_ref, wgv_ref, wgate_ref, walr_ref, wa2_ref, ba2_ref,
                    wsq_ref, wsk_ref, wsv_ref, qg_ref, kg_ref,
                    gq_ref, gk_ref, la_ref, gv_ref, gate_ref, sq_ref, skf_ref, skb_ref, svf_ref, svb_ref):
    xn = _rmsnorm_rows(x_ref[...], g1_ref[...]).astype(BF16)

    qk = _dot(xn, wqk_ref[...])
    gq_ref[...] = qk[:, :GLA_KEY] * (GLA_DK ** -0.5)
    gk_ref[...] = qk[:, GLA_KEY:]
    gv_ref[...] = _dot(xn, wgv_ref[...]).astype(BF16)
    gate_ref[...] = _dot(xn, wgate_ref[...])

    alr = _dot(xn, walr_ref[...])
    y = _dot(alr.astype(BF16), wa2_ref[...]) + ba2_ref[...]
    la_ref[...] = (jnp.minimum(y, 0.0) - jnp.log(1.0 + jnp.exp(-jnp.abs(y)))) * (1.0 / GATE_NORM)

    r = _iota((LANES, LANES), 0) // SB_HD
    c = _iota((LANES, LANES), 1) // SB_HD
    ones_bd = jnp.where(r == c, 1.0, 0.0).astype(BF16)

    def head_norm(s, g):
        outs = []
        for j in range(SB_WIDTH // LANES):
            sj = s[:, j * LANES:(j + 1) * LANES]
            hi, lo = _split2(sj * sj)
            ss = (_dot(hi, ones_bd) + _dot(lo, ones_bd)) * (1.0 / SB_HD)
            outs.append(sj * lax.rsqrt(ss + EPS))
        return jnp.concatenate(outs, axis=-1) * g

    sq = head_norm(_dot(xn, wsq_ref[...]), qg_ref[...])
    sq_ref[...] = (sq * (SB_HD ** -0.5)).astype(BF16)
    sk = head_norm(_dot(xn, wsk_ref[...]), kg_ref[...])
    skf_ref[...] = sk
    skb_ref[...] = sk.astype(BF16)
    sv = _dot(xn, wsv_ref[...])
    svf_ref[...] = sv
    svb_ref[...] = sv.astype(BF16)


def _project(x2d, lw, tm):
    n = x2d.shape[0]
    row = lambda w: pl.BlockSpec((tm, w), lambda i: (i, 0))
    full = lambda a: pl.BlockSpec(a.shape, lambda i: (0,) * a.ndim)
    weights = (lw['g1'], lw['wqk'], lw['wgv'], lw['wgate'], lw['walr'], lw['wa2'], lw['ba2'],
               lw['wsq'], lw['wsk'], lw['wsv'], lw['qg'], lw['kg'])
    out_cols = ((GLA_KEY, F32), (GLA_KEY, F32), (GLA_KEY, F32), (GLA_WIDTH, BF16), (GLA_WIDTH, F32),
                (SB_WIDTH, BF16), (SB_WIDTH, F32), (SB_WIDTH, BF16), (SB_WIDTH, F32), (SB_WIDTH, BF16))
    return pl.pallas_call(
        _project_kernel,
        grid=(n // tm,),
        in_specs=[row(D_MODEL)] + [full(w) for w in weights],
        out_specs=[row(w) for w, _ in out_cols],
        out_shape=[jax.ShapeDtypeStruct((n, w), dt) for w, dt in out_cols],
        compiler_params=pltpu.CompilerParams(dimension_semantics=("parallel",), vmem_limit_bytes=VMEM_LIMIT),
        name="project",
    )(x2d, *weights)


def _gla_kernel(*refs, n_chunks, has_init):
    if has_init:
        q_ref, k_ref, la_ref, v_ref, s0_ref, o_ref, sout_ref, s_scr, b_scr = refs
    else:
        q_ref, k_ref, la_ref, v_ref, o_ref, sout_ref, s_scr, b_scr = refs
    g = pl.program_id(2)

    @pl.when(g == 0)
    def _():
        if has_init:
            s_scr[...] = s0_ref[...].reshape(2 * GLA_DK, GLA_DV)
        else:
            s_scr[...] = jnp.zeros_like(s_scr)

    rr = _iota((CHUNK, CHUNK), 0)
    cc = _iota((CHUNK, CHUNK), 1)
    tri = jnp.where(cc <= rr, 1.0, 0.0).astype(BF16)
    r = _iota((LANES, LANES), 0) // GLA_DK
    c = _iota((LANES, LANES), 1) // GLA_DK
    ones_bd = jnp.where(r == c, 1.0, 0.0).astype(BF16)
    lane = _iota((CHUNK, LANES), 1)
    lane_t = lane % CHUNK
    srow = _iota((CHUNK, LANES), 0)
    head_a = lane < GLA_DK

    def chunk_body(ci, carry):
        base = pl.multiple_of(ci * CHUNK, CHUNK)
        rows = pl.ds(base, CHUNK)
        la = la_ref[rows, :]
        p1, p2, p3 = _split3(la)
        b = _dot(tri, p1) + _dot(tri, p2) + _dot(tri, p3)
        b_scr[...] = b
        q = q_ref[rows, :]
        k = k_ref[rows, :]
        v2 = v_ref[rows, :]

        def t_body(t, pt):
            brow = b_scr[pl.ds(t, 1), :]
            qrow = q_ref[pl.ds(base + t, 1), :]
            e = jnp.exp(brow - b) * (qrow * k)
            red = _dot(e.astype(BF16), ones_bd)
            return jnp.where(lane_t == t, red, pt)

        pt = lax.fori_loop(0, CHUNK, t_body, jnp.zeros((CHUNK, LANES), F32))
        pt = jnp.where(srow <= lane_t, pt, 0.0)
        o_intra = _dot(pt.T.astype(BF16), v2)

        s_bf = s_scr[...].astype(BF16)
        qe = q * jnp.exp(b)
        o_a = o_intra[:CHUNK, :GLA_DV] + _dot(jnp.where(head_a, qe, 0.0).astype(BF16), s_bf)
        o_b = o_intra[CHUNK:, GLA_DV:] + _dot(jnp.where(head_a, 0.0, qe).astype(BF16), s_bf)
        o_ref[rows, :GLA_DV] = o_a
        o_ref[rows, GLA_DV:] = o_b

        b_t = b.T
        b_end = b_t[:, CHUNK - 1:CHUNK]
        kd_t = (k.T * jnp.exp(b_end - b_t)).astype(BF16)
        upd = _dot(kd_t, v2)
        upd = jnp.concatenate([upd[:GLA_DK, :GLA_DV], upd[GLA_DK:, GLA_DV:]], axis=0)
        s_scr[...] = jnp.exp(b_end) * s_scr[...] + upd
        return carry

    lax.fori_loop(0, n_chunks, chunk_body, 0)

    @pl.when(g == pl.num_programs(2) - 1)
    def _():
        sout_ref[...] = s_scr[...].reshape(2, GLA_DK, GLA_DV)


def _gla(gq, gk, la, gv, s0, tg):
    bsz, t, _ = gq.shape
    has_init = s0 is not None
    n_chunks = tg // CHUNK
    qspec = pl.BlockSpec((None, tg, LANES), lambda b, p, g: (b, g, p))
    vspec = pl.BlockSpec((None, tg, 2 * GLA_DV), lambda b, p, g: (b, g, p))
    sspec = pl.BlockSpec((None, 2, GLA_DK, GLA_DV), lambda b, p, g: (b, p, 0, 0))
    in_specs = [qspec, qspec, qspec, vspec] + ([sspec] if has_init else [])
    args = (gq, gk, la, gv) + ((s0,) if has_init else ())
    return pl.pallas_call(
        functools.partial(_gla_kernel, n_chunks=n_chunks, has_init=has_init),
        grid=(bsz, GLA_HEADS // 2, t // tg),
        in_specs=in_specs,
        out_specs=[vspec, sspec],
        out_shape=[jax.ShapeDtypeStruct((bsz, t, GLA_WIDTH), F32),
                   jax.ShapeDtypeStruct((bsz, GLA_HEADS, GLA_DK, GLA_DV), F32)],
        scratch_shapes=[pltpu.VMEM((2 * GLA_DK, GLA_DV), F32), pltpu.VMEM((CHUNK, LANES), F32)],
        compiler_params=pltpu.CompilerParams(
            dimension_semantics=("parallel", "parallel", "arbitrary"), vmem_limit_bytes=VMEM_LIMIT),
        name="gla",
    )(*args)


def _cumsum_matrix(tk):
    r = _iota((tk, 2 * tk), 0)
    c = _iota((tk, 2 * tk), 1)
    return jnp.where((c >= tk) | (r > c), 1.0, 0.0).astype(BF16)


def _sb_kernel(*refs, tq, n_past_blocks):
    if n_past_blocks:
        q_ref, k_ref, v_ref, pk_ref, pv_ref, o_ref = refs
    else:
        q_ref, k_ref, v_ref, o_ref = refs
    i = pl.program_id(2)
    q = q_ref[...]
    lane = _iota((tq, LANES), 1)
    u_own = _cumsum_matrix(tq)
    strict = _iota((tq, tq), 1) < _iota((tq, tq), 0)

    def block(qh, kj, vj, u, carry, acc, mask):
        tk = kj.shape[0]
        z = _dot_nt(qh, kj)
        sp = _softplus(z)
        if mask is not None:
            sp = jnp.where(mask, sp, 0.0)
        hi, lo = _split2(sp)
        cr = _dot(hi, u) + _dot(lo, u)
        w = jnp.exp(z - sp - cr[:, :tk] - carry[:, :tk])
        if mask is not None:
            w = jnp.where(mask, w, 0.0)
        acc = acc + _dot(w.astype(BF16), vj)
        rs = cr[:, tk:]
        if tk < LANES:
            rs = jnp.concatenate([rs] * (LANES // tk), axis=1)
        return carry + rs, acc

    accs = []
    for h in range(2):
        in_head = (lane < SB_HD) if h == 0 else (lane >= SB_HD)
        qh = jnp.where(in_head, q, jnp.zeros_like(q))
        zeros = jnp.zeros((tq, LANES), F32)
        own = pl.ds(pl.multiple_of(i * tq, tq), tq)
        carry, acc = block(qh, k_ref[own, :], v_ref[own, :], u_own, zeros, zeros, strict)

        def own_body(n, ca):
            rows = pl.ds(pl.multiple_of((i - 1 - n) * tq, tq), tq)
            return block(qh, k_ref[rows, :], v_ref[rows, :], u_own, ca[0], ca[1], None)

        carry, acc = lax.fori_loop(0, i, own_body, (carry, acc))

        if n_past_blocks:
            u_past = _cumsum_matrix(LANES)

            def past_body(n, ca):
                rows = pl.ds(pl.multiple_of((n_past_blocks - 1 - n) * LANES, LANES), LANES)
                return block(qh, pk_ref[rows, :].astype(BF16), pv_ref[rows, :].astype(BF16), u_past,
                             ca[0], ca[1], None)

            carry, acc = lax.fori_loop(0, n_past_blocks, past_body, (carry, acc))
        accs.append(acc)
    o_ref[...] = jnp.where(lane < SB_HD, accs[0], accs[1]).astype(o_ref.dtype)


def _sb(q, k, v, past_k, past_v, tq):
    bsz, t, _ = q.shape
    n_past_blocks = 0 if past_k is None else past_k.shape[1] // LANES
    qspec = pl.BlockSpec((None, tq, LANES), lambda b, p, i: (b, i, p))
    kspec = pl.BlockSpec((None, t, LANES), lambda b, p, i: (b, 0, p))
    in_specs = [qspec, kspec, kspec]
    args = (q, k, v)
    if n_past_blocks:
        pspec = pl.BlockSpec((None, past_k.shape[1], LANES), lambda b, p, i: (b, 0, p))
        in_specs += [pspec, pspec]
        args += (past_k, past_v)
    return pl.pallas_call(
        functools.partial(_sb_kernel, tq=tq, n_past_blocks=n_past_blocks),
        grid=(bsz, SB_WIDTH // LANES, t // tq),
        in_specs=in_specs,
        out_specs=qspec,
        out_shape=jax.ShapeDtypeStruct((bsz, t, SB_WIDTH), BF16),
        compiler_params=pltpu.CompilerParams(
            dimension_semantics=("parallel", "parallel", "arbitrary"), vmem_limit_bytes=VMEM_LIMIT),
        name="sb",
    )(*args)


def _finish_kernel(x_ref, og_ref, gate_ref, osb_ref, gg_ref, wog_ref, wos_ref, g2_ref, wup_ref, wdn_ref, y_ref,
                   *, ff_block):
    og = og_ref[...]
    normed = [_rmsnorm_rows(og[:, h * GLA_DV:(h + 1) * GLA_DV], gg_ref[...]) for h in range(GLA_HEADS)]
    gate = gate_ref[...]
    mix_g = (jnp.concatenate(normed, axis=-1) * (gate * jax.nn.sigmoid(gate))).astype(BF16)
    h = x_ref[...] + _dot(mix_g, wog_ref[...]) + _dot(osb_ref[...], wos_ref[...])
    hn = _rmsnorm_rows(h, g2_ref[...]).astype(BF16)
    mlp = None
    for j in range(D_FF // ff_block):
        cols = slice(j * ff_block, (j + 1) * ff_block)
        u = _dot(hn, wup_ref[:, cols])
        a = jnp.square(jnp.maximum(u, 0.0)).astype(BF16)
        d = _dot(a, wdn_ref[cols, :])
        mlp = d if mlp is None else mlp + d
    y_ref[...] = h + mlp


def _finish(x2d, og, gate, osb, lw, tm):
    n = x2d.shape[0]
    row = lambda w: pl.BlockSpec((tm, w), lambda i: (i, 0))
    full = lambda a: pl.BlockSpec(a.shape, lambda i: (0,) * a.ndim)
    weights = (lw['gg'], lw['wog'], lw['wos'], lw['g2'], lw['wup'], lw['wdn'])
    return pl.pallas_call(
        functools.partial(_finish_kernel, ff_block=1024),
        grid=(n // tm,),
        in_specs=[row(D_MODEL), row(GLA_WIDTH), row(GLA_WIDTH), row(SB_WIDTH)] + [full(w) for w in weights],
        out_specs=row(D_MODEL),
        out_shape=jax.ShapeDtypeStruct((n, D_MODEL), F32),
        compiler_params=pltpu.CompilerParams(dimension_semantics=("parallel",), vmem_limit_bytes=VMEM_LIMIT),
        name="finish",
    )(x2d, og, gate, osb, *weights)


def _layer_weights(l, norm1_g, w_in, w_a2, b_a2, q_norm_g, k_norm_g, gla_norm_g, w_out, norm2_g, w_up, w_down):
    w = w_in[l].astype(BF16)
    o = 0
    cols = {}
    for name, width in (('gq', GLA_KEY), ('gk', GLA_KEY), ('gv', GLA_WIDTH), ('gate', GLA_WIDTH),
                        ('alr', GATE_RANK), ('sq', SB_WIDTH), ('sk', SB_WIDTH), ('sv', SB_WIDTH)):
        cols[name] = w[:, o:o + width]
        o += width
    wo = w_out[l].astype(BF16)
    return {
        'g1': norm1_g[l].reshape(1, D_MODEL),
        'wqk': jnp.concatenate([cols['gq'], cols['gk']], axis=1),
        'wgv': cols['gv'],
        'wgate': cols['gate'],
        'walr': jnp.pad(cols['alr'], ((0, 0), (0, LANES - GATE_RANK))),
        'wa2': jnp.pad(w_a2[l].astype(BF16), ((0, LANES - GATE_RANK), (0, 0))),
        'ba2': b_a2[l].reshape(1, GLA_KEY),
        'wsq': cols['sq'], 'wsk': cols['sk'], 'wsv': cols['sv'],
        'qg': jnp.tile(q_norm_g[l], SB_HEADS).reshape(1, SB_WIDTH),
        'kg': jnp.tile(k_norm_g[l], SB_HEADS).reshape(1, SB_WIDTH),
        'gg': gla_norm_g[l].reshape(1, GLA_DV),
        'wog': wo[:GLA_WIDTH], 'wos': wo[GLA_WIDTH:],
        'g2': norm2_g[l].reshape(1, D_MODEL),
        'wup': w_up[l].astype(BF16), 'wdn': w_down[l].astype(BF16),
    }


def _stream_layer(x, lw, past_k, past_v, s0, tm, tf, tg, tq):
    bsz, t, _ = x.shape
    x2d = x.reshape(bsz * t, D_MODEL)
    gq, gk, la, gv, gate, sq, skf, skb, svf, svb = _project(x2d, lw, tm)
    r3 = lambda a: a.reshape(bsz, t, a.shape[-1])
    o_gla, s_new = _gla(r3(gq), r3(gk), r3(la), r3(gv), s0, tg)
    o_sb = _sb(r3(sq), r3(skb), r3(svb), past_k, past_v, tq)
    y = _finish(x2d, o_gla.reshape(bsz * t, GLA_WIDTH), gate, o_sb.reshape(bsz * t, SB_WIDTH), lw, tf)
    heads = lambda a: a.reshape(bsz, t, SB_HEADS, SB_HD)
    return y.reshape(bsz, t, D_MODEL), heads(skf), heads(svf), s_new


def kernel(x_prompt, x_sample, cache_sb_k, cache_sb_v, state_gla, norm1_g, w_in, w_a2, b_a2, q_norm_g, k_norm_g,
           gla_norm_g, w_out, norm2_g, w_up, w_down):
    xp, xs = x_prompt, x_sample
    nb, past = cache_sb_k.shape[1], cache_sb_k.shape[2]
    outs = [[] for _ in range(6)]
    for l in range(DEPTH):
        lw = _layer_weights(l, norm1_g, w_in, w_a2, b_a2, q_norm_g, k_norm_g, gla_norm_g, w_out, norm2_g,
                            w_up, w_down)
        xp, kp, vp, sp = _stream_layer(xp, lw, None, None, None, tm=512, tf=256, tg=512, tq=128)
        xs, ks, vs, ss = _stream_layer(
            xs, lw, cache_sb_k[l].reshape(nb, past, SB_WIDTH), cache_sb_v[l].reshape(nb, past, SB_WIDTH),
            state_gla[l], tm=512, tf=256, tg=CHUNK, tq=CHUNK)
        for lst, a in zip(outs, (kp, vp, sp, ks, vs, ss)):
            lst.append(a)
    return (xp, xs) + tuple(jnp.stack(lst) for lst in outs)
```

```python
import functools

import jax
import jax.numpy as jnp
from jax import lax
from jax.experimental import pallas as pl
from jax.experimental.pallas import tpu as pltpu

D_MODEL = 1024
DEPTH = 2
CHUNK = 64
GLA_HEADS = 4
GLA_DK = 64
GLA_DV = 128
GLA_KEY = GLA_HEADS * GLA_DK
GLA_WIDTH = GLA_HEADS * GLA_DV
GATE_RANK = 16
GATE_NORM = 16.0
SB_HEADS = 8
SB_HD = 64
SB_WIDTH = SB_HEADS * SB_HD
D_FF = 4 * D_MODEL
EPS = 1e-6

LANES = 128
ROW_TILE = 16
SB_UNIT = 4
SB_ZERO_LOG = 110.0
VMEM_LIMIT = 56 * 1024 * 1024

F32 = jnp.float32
BF16 = jnp.bfloat16


def _dot(a, b):
    return jnp.dot(a, b, preferred_element_type=F32)


def _dot_nt(a, b):
    return lax.dot_general(a, b, (((1,), (1,)), ((), ())), preferred_element_type=F32)


def _iota(shape, dim):
    return lax.broadcasted_iota(jnp.int32, shape, dim)


def _split3(x):
    p1 = x.astype(BF16)
    r1 = x - p1.astype(F32)
    p2 = r1.astype(BF16)
    p3 = (r1 - p2.astype(F32)).astype(BF16)
    return p1, p2, p3


def _split2(x):
    p1 = x.astype(BF16)
    p2 = (x - p1.astype(F32)).astype(BF16)
    return p1, p2


def _softplus(z):
    return jnp.maximum(z, 0.0) + jnp.log(1.0 + jnp.exp(-jnp.abs(z)))


def _rmsnorm_rows(x, g):
    ms = jnp.mean(x * x, axis=-1, keepdims=True)
    return x * lax.rsqrt(ms + EPS) * g


def _project_kernel(x_ref, g1_ref, wqk_ref, wgv_ref, wgate_ref, walr_ref, wa2_ref, ba2_ref,
                    wsq_ref, wsk_ref, wsv_ref, qg_ref, kg_ref,
                    gq_ref, gk_ref, la_ref, gv_ref, gate_ref, sq_ref, skf_ref, skb_ref, svf_ref, svb_ref):
    xn = _rmsnorm_rows(x_ref[...], g1_ref[...]).astype(BF16)

    qk = _dot(xn, wqk_ref[...])
    gq_ref[...] = qk[:, :GLA_KEY] * (GLA_DK ** -0.5)
    gk_ref[...] = qk[:, GLA_KEY:]
    gv_ref[...] = _dot(xn, wgv_ref[...]).astype(BF16)
    gate_ref[...] = _dot(xn, wgate_ref[...])

    alr = _dot(xn, walr_ref[...])
    y = _dot(alr.astype(BF16), wa2_ref[...]) + ba2_ref[...]
    la_ref[...] = (jnp.minimum(y, 0.0) - jnp.log(1.0 + jnp.exp(-jnp.abs(y)))) * (1.0 / GATE_NORM)

    r = _iota((LANES, LANES), 0) // SB_HD
    c = _iota((LANES, LANES), 1) // SB_HD
    ones_bd = jnp.where(r == c, 1.0, 0.0).astype(BF16)

    def head_norm(s, g):
        outs = []
        for j in range(SB_WIDTH // LANES):
            sj = s[:, j * LANES:(j + 1) * LANES]
            hi, lo = _split2(sj * sj)
            ss = (_dot(hi, ones_bd) + _dot(lo, ones_bd)) * (1.0 / SB_HD)
            outs.append(sj * lax.rsqrt(ss + EPS))
        return jnp.concatenate(outs, axis=-1) * g

    sq = head_norm(_dot(xn, wsq_ref[...]), qg_ref[...])
    sq_ref[...] = (sq * (SB_HD ** -0.5)).astype(BF16)
    sk = head_norm(_dot(xn, wsk_ref[...]), kg_ref[...])
    skf_ref[...] = sk
    skb_ref[...] = sk.astype(BF16)
    sv = _dot(xn, wsv_ref[...])
    svf_ref[...] = sv
    svb_ref[...] = sv.astype(BF16)


def _project(x2d, lw, tm):
    n = x2d.shape[0]
    row = lambda w: pl.BlockSpec((tm, w), lambda i: (i, 0))
    full = lambda a: pl.BlockSpec(a.shape, lambda i: (0,) * a.ndim)
    weights = (lw['g1'], lw['wqk'], lw['wgv'], lw['wgate'], lw['walr'], lw['wa2'], lw['ba2'],
               lw['wsq'], lw['wsk'], lw['wsv'], lw['qg'], lw['kg'])
    out_cols = ((GLA_KEY, F32), (GLA_KEY, F32), (GLA_KEY, F32), (GLA_WIDTH, BF16), (GLA_WIDTH, F32),
                (SB_WIDTH, BF16), (SB_WIDTH, F32), (SB_WIDTH, BF16), (SB_WIDTH, F32), (SB_WIDTH, BF16))
    return pl.pallas_call(
        _project_kernel,
        grid=(n // tm,),
        in_specs=[row(D_MODEL)] + [full(w) for w in weights],
        out_specs=[row(w) for w, _ in out_cols],
        out_shape=[jax.ShapeDtypeStruct((n, w), dt) for w, dt in out_cols],
        compiler_params=pltpu.CompilerParams(dimension_semantics=("parallel",), vmem_limit_bytes=VMEM_LIMIT),
        name="project",
    )(x2d, *weights)


def _gla_kernel(*refs, n_chunks, has_init):
    if has_init:
        q_ref, k_ref, la_ref, v_ref, s0_ref, o_ref, sout_ref, s_scr, b_scr = refs
    else:
        q_ref, k_ref, la_ref, v_ref, o_ref, sout_ref, s_scr, b_scr = refs
    g = pl.program_id(2)

    @pl.when(g == 0)
    def _():
        if has_init:
            s_scr[...] = s0_ref[...].reshape(2 * GLA_DK, GLA_DV)
        else:
            s_scr[...] = jnp.zeros_like(s_scr)

    rr = _iota((CHUNK, CHUNK), 0)
    cc = _iota((CHUNK, CHUNK), 1)
    tri = jnp.where(cc <= rr, 1.0, 0.0).astype(BF16)
    r = _iota((LANES, LANES), 0) // GLA_DK
    c = _iota((LANES, LANES), 1) // GLA_DK
    ones_bd = jnp.where(r == c, 1.0, 0.0).astype(BF16)
    lane = _iota((CHUNK, LANES), 1)
    lane_t = lane % CHUNK
    tile_t = _iota((ROW_TILE, LANES), 1) % CHUNK
    srow = _iota((CHUNK, LANES), 0)
    head_a = lane < GLA_DK

    def chunk_body(ci, carry):
        base = pl.multiple_of(ci * CHUNK, CHUNK)
        rows = pl.ds(base, CHUNK)
        la = la_ref[rows, :]
        p1, p2, p3 = _split3(la)
        b = _dot(tri, p1) + _dot(tri, p2) + _dot(tri, p3)
        b_scr[...] = b
        q = q_ref[rows, :]
        k = k_ref[rows, :]
        v2 = v_ref[rows, :]

        pt_tiles = [jnp.zeros((ROW_TILE, LANES), F32) for _ in range(CHUNK // ROW_TILE)]
        for t in range(CHUNK):
            n_tiles = t // ROW_TILE + 1
            s_hi = n_tiles * ROW_TILE
            brow = b_scr[t:t + 1, :]
            qrow = q_ref[pl.ds(base + t, 1), :]
            e = jnp.exp(brow - b[:s_hi]) * (qrow * k[:s_hi])
            red = _dot(e.astype(BF16), ones_bd)
            for i in range(n_tiles):
                pt_tiles[i] = jnp.where(tile_t == t, red[i * ROW_TILE:(i + 1) * ROW_TILE], pt_tiles[i])
        pt = jnp.concatenate(pt_tiles, axis=0)
        pt = jnp.where(srow <= lane_t, pt, 0.0)
        o_intra = _dot(pt.T.astype(BF16), v2)

        s_bf = s_scr[...].astype(BF16)
        qe = q * jnp.exp(b)
        o_a = o_intra[:CHUNK, :GLA_DV] + _dot(jnp.where(head_a, qe, 0.0).astype(BF16), s_bf)
        o_b = o_intra[CHUNK:, GLA_DV:] + _dot(jnp.where(head_a, 0.0, qe).astype(BF16), s_bf)
        o_ref[rows, :GLA_DV] = o_a
        o_ref[rows, GLA_DV:] = o_b

        b_t = b.T
        b_end = b_t[:, CHUNK - 1:CHUNK]
        kd_t = (k.T * jnp.exp(b_end - b_t)).astype(BF16)
        upd = _dot(kd_t, v2)
        upd = jnp.concatenate([upd[:GLA_DK, :GLA_DV], upd[GLA_DK:, GLA_DV:]], axis=0)
        s_scr[...] = jnp.exp(b_end) * s_scr[...] + upd
        return carry

    lax.fori_loop(0, n_chunks, chunk_body, 0)

    @pl.when(g == pl.num_programs(2) - 1)
    def _():
        sout_ref[...] = s_scr[...].reshape(2, GLA_DK, GLA_DV)


def _gla(gq, gk, la, gv, s0, tg):
    bsz, t, _ = gq.shape
    has_init = s0 is not None
    n_chunks = tg // CHUNK
    qspec = pl.BlockSpec((None, tg, LANES), lambda b, p, g: (b, g, p))
    vspec = pl.BlockSpec((None, tg, 2 * GLA_DV), lambda b, p, g: (b, g, p))
    sspec = pl.BlockSpec((None, 2, GLA_DK, GLA_DV), lambda b, p, g: (b, p, 0, 0))
    in_specs = [qspec, qspec, qspec, vspec] + ([sspec] if has_init else [])
    args = (gq, gk, la, gv) + ((s0,) if has_init else ())
    return pl.pallas_call(
        functools.partial(_gla_kernel, n_chunks=n_chunks, has_init=has_init),
        grid=(bsz, GLA_HEADS // 2, t // tg),
        in_specs=in_specs,
        out_specs=[vspec, sspec],
        out_shape=[jax.ShapeDtypeStruct((bsz, t, GLA_WIDTH), F32),
                   jax.ShapeDtypeStruct((bsz, GLA_HEADS, GLA_DK, GLA_DV), F32)],
        scratch_shapes=[pltpu.VMEM((2 * GLA_DK, GLA_DV), F32), pltpu.VMEM((CHUNK, LANES), F32)],
        compiler_params=pltpu.CompilerParams(
            dimension_semantics=("parallel", "parallel", "arbitrary"), vmem_limit_bytes=VMEM_LIMIT),
        name="gla",
    )(*args)


def _cumsum_matrix(tk):
    r = _iota((tk, 2 * tk), 0)
    c = _iota((tk, 2 * tk), 1)
    return jnp.where((c >= tk) | (r > c), 1.0, 0.0).astype(BF16)


def _sb_chain(qh, kj, vj, u, carry, acc, mask):
    tk = kj.shape[0]
    z = _dot_nt(qh, kj)
    sp = _softplus(z)
    if mask is not None:
        sp = jnp.where(mask, sp, 0.0)
    hi, lo = _split2(sp)
    cr = _dot(hi, u) + _dot(lo, u)
    w = jnp.exp(z - sp - cr[:, :tk] - carry[:, :tk])
    if mask is not None:
        w = jnp.where(mask, w, 0.0)
    acc = acc + _dot(w.astype(BF16), vj)
    rs = cr[:, tk:]
    if tk < LANES:
        rs = jnp.concatenate([rs] * (LANES // tk), axis=1)
    return carry + rs, acc


def _sb_kernel(*refs, tq, n_past_blocks):
    if n_past_blocks:
        q_ref, k_ref, v_ref, pk_ref, pv_ref, o_ref = refs
    else:
        q_ref, k_ref, v_ref, o_ref = refs
    i = pl.program_id(2)
    q = q_ref[...]
    lane = _iota((tq, LANES), 1)
    u_own = _cumsum_matrix(tq)
    strict = _iota((tq, tq), 1) < _iota((tq, tq), 0)
    q_heads = [jnp.where(lane < SB_HD, q, jnp.zeros_like(q)), jnp.where(lane < SB_HD, jnp.zeros_like(q), q)]
    zeros = jnp.zeros((tq, LANES), F32)

    def unit(blocks, carries, accs):
        carries, accs = list(carries), list(accs)
        for kj, vj, u, mask in blocks:
            for h in range(2):
                carries[h], accs[h] = _sb_chain(q_heads[h], kj, vj, u, carries[h], accs[h], mask)
        return carries, accs

    if n_past_blocks:
        u_past = _cumsum_matrix(LANES)
        blocks = [(k_ref[...], v_ref[...], u_own, strict)]
        for j in reversed(range(n_past_blocks)):
            rows = slice(j * LANES, (j + 1) * LANES)
            blocks.append((pk_ref[rows, :].astype(BF16), pv_ref[rows, :].astype(BF16), u_past, None))
        _, accs = unit(blocks, (zeros, zeros), (zeros, zeros))
    else:
        def load_unit(n, first):
            blocks = []
            for kk in range(SB_UNIT):
                j = i - n * SB_UNIT - kk
                rows = pl.ds(pl.multiple_of(jnp.maximum(j, 0) * tq, tq), tq)
                vj = v_ref[rows, :]
                diag = first and kk == 0
                if not diag:
                    vj = jnp.where(j >= 0, vj, jnp.zeros_like(vj))
                blocks.append((k_ref[rows, :], vj, u_own, strict if diag else None))
            return blocks

        def min_carry(carries):
            return jnp.min(jnp.minimum(carries[0], carries[1]))

        carries, accs = unit(load_unit(0, True), (zeros, zeros), (zeros, zeros))
        n_units = (i + SB_UNIT) // SB_UNIT

        def cond(st):
            return (st[0] < n_units) & (st[1] < SB_ZERO_LOG)

        def body(st):
            n, _, ca, cb, aa, ab = st
            (ca, cb), (aa, ab) = unit(load_unit(n, False), (ca, cb), (aa, ab))
            return n + 1, min_carry((ca, cb)), ca, cb, aa, ab

        st = lax.while_loop(cond, body, (jnp.int32(1), min_carry(carries), *carries, *accs))
        accs = st[4:]
    o_ref[...] = jnp.where(lane < SB_HD, accs[0], accs[1]).astype(o_ref.dtype)


def _sb(q, k, v, past_k, past_v, tq):
    bsz, t, _ = q.shape
    n_past_blocks = 0 if past_k is None else past_k.shape[1] // LANES
    qspec = pl.BlockSpec((None, tq, LANES), lambda b, p, i: (b, i, p))
    kspec = pl.BlockSpec((None, t, LANES), lambda b, p, i: (b, 0, p))
    in_specs = [qspec, kspec, kspec]
    args = (q, k, v)
    if n_past_blocks:
        pspec = pl.BlockSpec((None, past_k.shape[1], LANES), lambda b, p, i: (b, 0, p))
        in_specs += [pspec, pspec]
        args += (past_k, past_v)
    return pl.pallas_call(
        functools.partial(_sb_kernel, tq=tq, n_past_blocks=n_past_blocks),
        grid=(bsz, SB_WIDTH // LANES, t // tq),
        in_specs=in_specs,
        out_specs=qspec,
        out_shape=jax.ShapeDtypeStruct((bsz, t, SB_WIDTH), BF16),
        compiler_params=pltpu.CompilerParams(
            dimension_semantics=("parallel", "parallel", "arbitrary"), vmem_limit_bytes=VMEM_LIMIT),
        name="sb",
    )(*args)


def _finish_kernel(x_ref, og_ref, gate_ref, osb_ref, gg_ref, wog_ref, wos_ref, g2_ref, wup_ref, wdn_ref, y_ref,
                   *, ff_block):
    og = og_ref[...]
    normed = [_rmsnorm_rows(og[:, h * GLA_DV:(h + 1) * GLA_DV], gg_ref[...]) for h in range(GLA_HEADS)]
    gate = gate_ref[...]
    mix_g = (jnp.concatenate(normed, axis=-1) * (gate * jax.nn.sigmoid(gate))).astype(BF16)
    h = x_ref[...] + _dot(mix_g, wog_ref[...]) + _dot(osb_ref[...], wos_ref[...])
    hn = _rmsnorm_rows(h, g2_ref[...]).astype(BF16)
    mlp = None
    for j in range(D_FF // ff_block):
        cols = slice(j * ff_block, (j + 1) * ff_block)
        u = _dot(hn, wup_ref[:, cols])
        a = jnp.square(jnp.maximum(u, 0.0)).astype(BF16)
        d = _dot(a, wdn_ref[cols, :])
        mlp = d if mlp is None else mlp + d
    y_ref[...] = h + mlp


def _finish(x2d, og, gate, osb, lw, tm):
    n = x2d.shape[0]
    row = lambda w: pl.BlockSpec((tm, w), lambda i: (i, 0))
    full = lambda a: pl.BlockSpec(a.shape, lambda i: (0,) * a.ndim)
    weights = (lw['gg'], lw['wog'], lw['wos'], lw['g2'], lw['wup'], lw['wdn'])
    return pl.pallas_call(
        functools.partial(_finish_kernel, ff_block=1024),
        grid=(n // tm,),
        in_specs=[row(D_MODEL), row(GLA_WIDTH), row(GLA_WIDTH), row(SB_WIDTH)] + [full(w) for w in weights],
        out_specs=row(D_MODEL),
        out_shape=jax.ShapeDtypeStruct((n, D_MODEL), F32),
        compiler_params=pltpu.CompilerParams(dimension_semantics=("parallel",), vmem_limit_bytes=VMEM_LIMIT),
        name="finish",
    )(x2d, og, gate, osb, *weights)


def _layer_weights(l, norm1_g, w_in, w_a2, b_a2, q_norm_g, k_norm_g, gla_norm_g, w_out, norm2_g, w_up, w_down):
    w = w_in[l].astype(BF16)
    o = 0
    cols = {}
    for name, width in (('gq', GLA_KEY), ('gk', GLA_KEY), ('gv', GLA_WIDTH), ('gate', GLA_WIDTH),
                        ('alr', GATE_RANK), ('sq', SB_WIDTH), ('sk', SB_WIDTH), ('sv', SB_WIDTH)):
        cols[name] = w[:, o:o + width]
        o += width
    wo = w_out[l].astype(BF16)
    return {
        'g1': norm1_g[l].reshape(1, D_MODEL),
        'wqk': jnp.concatenate([cols['gq'], cols['gk']], axis=1),
        'wgv': cols['gv'],
        'wgate': cols['gate'],
        'walr': jnp.pad(cols['alr'], ((0, 0), (0, LANES - GATE_RANK))),
        'wa2': jnp.pad(w_a2[l].astype(BF16), ((0, LANES - GATE_RANK), (0, 0))),
        'ba2': b_a2[l].reshape(1, GLA_KEY),
        'wsq': cols['sq'], 'wsk': cols['sk'], 'wsv': cols['sv'],
        'qg': jnp.tile(q_norm_g[l], SB_HEADS).reshape(1, SB_WIDTH),
        'kg': jnp.tile(k_norm_g[l], SB_HEADS).reshape(1, SB_WIDTH),
        'gg': gla_norm_g[l].reshape(1, GLA_DV),
        'wog': wo[:GLA_WIDTH], 'wos': wo[GLA_WIDTH:],
        'g2': norm2_g[l].reshape(1, D_MODEL),
        'wup': w_up[l].astype(BF16), 'wdn': w_down[l].astype(BF16),
    }


def _stream_layer(x, lw, past_k, past_v, s0, tm, tf, tg, tq):
    bsz, t, _ = x.shape
    x2d = x.reshape(bsz * t, D_MODEL)
    gq, gk, la, gv, gate, sq, skf, skb, svf, svb = _project(x2d, lw, tm)
    r3 = lambda a: a.reshape(bsz, t, a.shape[-1])
    o_gla, s_new = _gla(r3(gq), r3(gk), r3(la), r3(gv), s0, tg)
    o_sb = _sb(r3(sq), r3(skb), r3(svb), past_k, past_v, tq)
    y = _finish(x2d, o_gla.reshape(bsz * t, GLA_WIDTH), gate, o_sb.reshape(bsz * t, SB_WIDTH), lw, tf)
    heads = lambda a: a.reshape(bsz, t, SB_HEADS, SB_HD)
    return y.reshape(bsz, t, D_MODEL), heads(skf), heads(svf), s_new


def kernel(x_prompt, x_sample, cache_sb_k, cache_sb_v, state_gla, norm1_g, w_in, w_a2, b_a2, q_norm_g, k_norm_g,
           gla_norm_g, w_out, norm2_g, w_up, w_down):
    xp, xs = x_prompt, x_sample
    nb, past = cache_sb_k.shape[1], cache_sb_k.shape[2]
    outs = [[] for _ in range(6)]
    for l in range(DEPTH):
        lw = _layer_weights(l, norm1_g, w_in, w_a2, b_a2, q_norm_g, k_norm_g, gla_norm_g, w_out, norm2_g,
                            w_up, w_down)
        xp, kp, vp, sp = _stream_layer(xp, lw, None, None, None, tm=512, tf=256, tg=512, tq=128)
        xs, ks, vs, ss = _stream_layer(
            xs, lw, cache_sb_k[l].reshape(nb, past, SB_WIDTH), cache_sb_v[l].reshape(nb, past, SB_WIDTH),
            state_gla[l], tm=512, tf=256, tg=CHUNK, tq=CHUNK)
        for lst, a in zip(outs, (kp, vp, sp, ks, vs, ss)):
            lst.append(a)
    return (xp, xs) + tuple(jnp.stack(lst) for lst in outs)
```

```python
import functools

import jax
import jax.numpy as jnp
from jax import lax
from jax.experimental import pallas as pl
from jax.experimental.pallas import tpu as pltpu

D_MODEL = 1024
DEPTH = 2
CHUNK = 64
GLA_HEADS = 4
GLA_DK = 64
GLA_DV = 128
GLA_KEY = GLA_HEADS * GLA_DK
GLA_WIDTH = GLA_HEADS * GLA_DV
GATE_RANK = 16
GATE_NORM = 16.0
SB_HEADS = 8
SB_HD = 64
SB_WIDTH = SB_HEADS * SB_HD
D_FF = 4 * D_MODEL
EPS = 1e-6

LANES = 128
ROW_TILE = 16
SB_UNIT = 3
SB_ZERO_LOG = 110.0
LOG2E = 1.4426950408889634
VMEM_LIMIT = 56 * 1024 * 1024

_GATE_COLS = 2 * GLA_KEY + 2 * GLA_WIDTH
_COL_QK = (0, 2 * GLA_KEY)
_COL_GV = (2 * GLA_KEY, 2 * GLA_KEY + GLA_WIDTH)
_COL_GATE = (2 * GLA_KEY + GLA_WIDTH, _GATE_COLS)
_COL_ALR = (_GATE_COLS, _GATE_COLS + LANES)
_COL_SQ = (_GATE_COLS + LANES, _GATE_COLS + LANES + SB_WIDTH)
_COL_SK = (_COL_SQ[1], _COL_SQ[1] + SB_WIDTH)
_COL_SV = (_COL_SK[1], _COL_SK[1] + SB_WIDTH)
IN_COLS_PADDED = _COL_SV[1]

F32 = jnp.float32
BF16 = jnp.bfloat16


def _dot(a, b):
    return jnp.dot(a, b, preferred_element_type=F32)


def _dot_nt(a, b):
    return lax.dot_general(a, b, (((1,), (1,)), ((), ())), preferred_element_type=F32)


def _iota(shape, dim):
    return lax.broadcasted_iota(jnp.int32, shape, dim)


def _split3(x):
    p1 = x.astype(BF16)
    r1 = x - p1.astype(F32)
    p2 = r1.astype(BF16)
    p3 = (r1 - p2.astype(F32)).astype(BF16)
    return p1, p2, p3


def _split2(x):
    p1 = x.astype(BF16)
    p2 = (x - p1.astype(F32)).astype(BF16)
    return p1, p2


def _softplus(z):
    return jnp.maximum(z, 0.0) + jnp.log(1.0 + jnp.exp(-jnp.abs(z)))


def _rmsnorm_rows(x, g):
    ms = jnp.mean(x * x, axis=-1, keepdims=True)
    return x * lax.rsqrt(ms + EPS) * g


def _project_kernel(x_ref, g1_ref, win_ref, wa2_ref, ba2_ref, qg_ref, kg_ref,
                    gq_ref, gk_ref, la_ref, gv_ref, gate_ref, sq_ref, skb_ref, svb_ref, skf_ref, svf_ref):
    xn = _rmsnorm_rows(x_ref[...], g1_ref[...]).astype(BF16)
    proj = lambda cols: _dot(xn, win_ref[:, cols[0]:cols[1]])

    qk = proj(_COL_QK)
    gq_ref[...] = qk[:, :GLA_KEY] * (GLA_DK ** -0.5)
    gk_ref[...] = qk[:, GLA_KEY:]
    gv_ref[...] = proj(_COL_GV).astype(BF16)
    gate_ref[...] = proj(_COL_GATE)

    alr = proj(_COL_ALR)
    y = _dot(alr.astype(BF16), wa2_ref[...]) + ba2_ref[...]
    la_ref[...] = (jnp.minimum(y, 0.0) - jnp.log(1.0 + jnp.exp(-jnp.abs(y)))) * (1.0 / GATE_NORM)

    r = _iota((LANES, LANES), 0) // SB_HD
    c = _iota((LANES, LANES), 1) // SB_HD
    ones_bd = jnp.where(r == c, 1.0, 0.0).astype(BF16)

    def head_norm(s, g):
        outs = []
        for j in range(SB_WIDTH // LANES):
            sj = s[:, j * LANES:(j + 1) * LANES]
            hi, lo = _split2(sj * sj)
            ss = (_dot(hi, ones_bd) + _dot(lo, ones_bd)) * (1.0 / SB_HD)
            outs.append(sj * lax.rsqrt(ss + EPS))
        return jnp.concatenate(outs, axis=-1) * g

    sq = head_norm(proj(_COL_SQ), qg_ref[...])
    sq_ref[...] = (sq * (SB_HD ** -0.5)).astype(BF16)
    sk = head_norm(proj(_COL_SK), kg_ref[...])
    skb_ref[...] = sk.astype(BF16)
    skf_ref[...] = sk
    sv = proj(_COL_SV)
    svb_ref[...] = sv.astype(BF16)
    svf_ref[...] = sv


def _project(x2d, pw, layer, tm):
    n = x2d.shape[0]
    row = lambda w: pl.BlockSpec((tm, w), lambda i: (i, 0))
    per_layer = lambda a: pl.BlockSpec((None,) + a.shape[1:], lambda i: (layer,) + (0,) * (a.ndim - 1))
    weights = (pw['g1'], pw['win'], pw['wa2'], pw['ba2'], pw['qg'], pw['kg'])
    out_cols = ((GLA_KEY, F32), (GLA_KEY, F32), (GLA_KEY, F32), (GLA_WIDTH, BF16), (GLA_WIDTH, F32),
                (SB_WIDTH, BF16), (SB_WIDTH, BF16), (SB_WIDTH, BF16), (SB_WIDTH, F32), (SB_WIDTH, F32))
    return pl.pallas_call(
        _project_kernel,
        grid=(n // tm,),
        in_specs=[row(D_MODEL)] + [per_layer(w) for w in weights],
        out_specs=[row(w) for w, _ in out_cols],
        out_shape=[jax.ShapeDtypeStruct((n, w), dt) for w, dt in out_cols],
        compiler_params=pltpu.CompilerParams(dimension_semantics=("parallel",), vmem_limit_bytes=VMEM_LIMIT),
        name="project",
    )(x2d, *weights)


def _gla_kernel(*refs, n_chunks, chunks_per_step, has_init):
    if has_init:
        q_ref, k_ref, la_ref, v_ref, s0_ref, o_ref, sout_ref, s_scr, b_scr = refs
    else:
        q_ref, k_ref, la_ref, v_ref, o_ref, sout_ref, s_scr, b_scr = refs
    g = pl.program_id(2)

    @pl.when(g == 0)
    def _():
        if has_init:
            s_scr[...] = s0_ref[...].reshape(2 * GLA_DK, GLA_DV)
        else:
            s_scr[...] = jnp.zeros_like(s_scr)

    rr = _iota((CHUNK, CHUNK), 0)
    cc = _iota((CHUNK, CHUNK), 1)
    tri = jnp.where(cc <= rr, 1.0, 0.0).astype(BF16)
    r = _iota((2 * LANES, 2 * LANES), 0) // GLA_DK
    c = _iota((2 * LANES, 2 * LANES), 1) // GLA_DK
    ones_bd2 = jnp.where(r == c, 1.0, 0.0).astype(BF16)
    lane = _iota((CHUNK, LANES), 1)
    lane_t = lane % CHUNK
    tile_t = _iota((ROW_TILE, LANES), 1) % CHUNK
    srow = _iota((CHUNK, LANES), 0)
    head_a = lane < GLA_DK

    n_sub = CHUNK // ROW_TILE
    sub = lambda g: slice(g * ROW_TILE, (g + 1) * ROW_TILE)

    def operands(slot, base, b):
        b2 = b * LOG2E
        b_scr[slot] = b2
        rows = pl.ds(base, CHUNK)
        q = q_ref[rows, :]
        k = k_ref[rows, :]

        b_end = jnp.concatenate(
            [jnp.broadcast_to(b[(g + 1) * ROW_TILE - 1:(g + 1) * ROW_TILE], (ROW_TILE, LANES)) for g in range(n_sub)],
            axis=0)
        k_dec = (k * jnp.exp(b_end - b)).astype(BF16)
        q_dec_rhs = []
        for g in range(n_sub - 1):
            lo = (g + 1) * ROW_TILE
            q_dec = q[lo:] * jnp.exp(b[lo:] - b[lo - 1:lo])
            pad = jnp.zeros((lo, LANES), F32)
            in_a = _iota((CHUNK - lo, LANES), 1) < GLA_DK
            q_dec_rhs.append(
                jnp.concatenate([pad, jnp.where(in_a, q_dec, 0.0), pad, jnp.where(in_a, 0.0, q_dec)],
                                axis=0).astype(BF16))

        es = []
        for t in range(CHUNK):
            g = t // ROW_TILE
            brow = b_scr[slot, t:t + 1, :]
            qrow = q_ref[pl.ds(base + t, 1), :]
            es.append(jnp.exp2(brow - b2[sub(g)]).astype(BF16) * (qrow * k[sub(g)]).astype(BF16))
        diag_lhs = jnp.concatenate(
            [jnp.concatenate([es[2 * p], es[2 * p + 1]], axis=1) for p in range(CHUNK // 2)], axis=0)

        qe = q * jnp.exp(b)
        qe_a = jnp.where(head_a, qe, 0.0).astype(BF16)
        qe_b = jnp.where(head_a, 0.0, qe).astype(BF16)
        b_t = b.T
        b_last = b_t[:, CHUNK - 1:CHUNK]
        kd_t = (k.T * jnp.exp(b_last - b_t)).astype(BF16)
        return k_dec, q_dec_rhs, diag_lhs, qe_a, qe_b, kd_t, jnp.exp(b_last)

    def scores(off_tiles, red):
        pt_tiles = list(off_tiles) + [jnp.zeros((ROW_TILE, LANES), F32)]
        for t in range(CHUNK):
            g, p, half = t // ROW_TILE, t // 2, t % 2
            r_t = red[p * ROW_TILE:(p + 1) * ROW_TILE, half * LANES:(half + 1) * LANES]
            pt_tiles[g] = jnp.where(tile_t == t, r_t, pt_tiles[g])
        pt = jnp.concatenate(pt_tiles, axis=0)
        return jnp.where(srow <= lane_t, pt, 0.0)

    def step_body(si, carry):
        slots = range(chunks_per_step)
        bases = [pl.multiple_of((si * chunks_per_step + slot) * CHUNK, CHUNK) for slot in slots]
        bs = []
        for base in bases:
            p1, p2, p3 = _split3(la_ref[pl.ds(base, CHUNK), :])
            bs.append(_dot(tri, p1) + _dot(tri, p2) + _dot(tri, p3))
        ops = [operands(slot, bases[slot], bs[slot]) for slot in slots]
        v2s = [v_ref[pl.ds(base, CHUNK), :] for base in bases]
        offs = [[_dot_nt(op[0][sub(g)], op[1][g]) for g in range(n_sub - 1)] for op in ops]
        reds = [_dot(op[2], ones_bd2) for op in ops]
        upds = [_dot(op[5], v2) for op, v2 in zip(ops, v2s)]
        pts = [scores(off, red).T.astype(BF16) for off, red in zip(offs, reds)]
        o_intras = [_dot(pt, v2) for pt, v2 in zip(pts, v2s)]

        s = s_scr[...]
        for slot in slots:
            _, _, _, qe_a, qe_b, _, decay = ops[slot]
            rows = pl.ds(bases[slot], CHUNK)
            s_bf = s.astype(BF16)
            o_ref[rows, :GLA_DV] = o_intras[slot][:CHUNK, :GLA_DV] + _dot(qe_a, s_bf)
            o_ref[rows, GLA_DV:] = o_intras[slot][CHUNK:, GLA_DV:] + _dot(qe_b, s_bf)
            upd = upds[slot]
            s = decay * s + jnp.concatenate([upd[:GLA_DK, :GLA_DV], upd[GLA_DK:, GLA_DV:]], axis=0)
        s_scr[...] = s
        return carry

    lax.fori_loop(0, n_chunks // chunks_per_step, step_body, 0)

    @pl.when(g == pl.num_programs(2) - 1)
    def _():
        sout_ref[...] = s_scr[...].reshape(2, GLA_DK, GLA_DV)


def _gla(gq, gk, la, gv, s0, layer, tg):
    bsz, t, _ = gq.shape
    has_init = s0 is not None
    n_chunks = tg // CHUNK
    chunks_per_step = 8 if n_chunks % 8 == 0 else 1
    qspec = pl.BlockSpec((None, tg, LANES), lambda b, p, g: (b, g, p))
    vspec = pl.BlockSpec((None, tg, 2 * GLA_DV), lambda b, p, g: (b, g, p))
    sspec = pl.BlockSpec((None, 2, GLA_DK, GLA_DV), lambda b, p, g: (b, p, 0, 0))
    s0spec = pl.BlockSpec((None, None, 2, GLA_DK, GLA_DV), lambda b, p, g: (layer, b, p, 0, 0))
    in_specs = [qspec, qspec, qspec, vspec] + ([s0spec] if has_init else [])
    args = (gq, gk, la, gv) + ((s0,) if has_init else ())
    return pl.pallas_call(
        functools.partial(_gla_kernel, n_chunks=n_chunks, chunks_per_step=chunks_per_step, has_init=has_init),
        grid=(bsz, GLA_HEADS // 2, t // tg),
        in_specs=in_specs,
        out_specs=[vspec, sspec],
        out_shape=[jax.ShapeDtypeStruct((bsz, t, GLA_WIDTH), F32),
                   jax.ShapeDtypeStruct((bsz, GLA_HEADS, GLA_DK, GLA_DV), F32)],
        scratch_shapes=[pltpu.VMEM((2 * GLA_DK, GLA_DV), F32), pltpu.VMEM((chunks_per_step, CHUNK, LANES), F32)],
        compiler_params=pltpu.CompilerParams(
            dimension_semantics=("parallel", "parallel", "arbitrary"), vmem_limit_bytes=VMEM_LIMIT),
        name="gla",
    )(*args)


def _cumsum_matrix(tk):
    r = _iota((tk, 2 * tk), 0)
    c = _iota((tk, 2 * tk), 1)
    return jnp.where((c >= tk) | (r > c), 1.0, 0.0).astype(BF16)


def _sb_unit(streams):
    zs = [[_dot_nt(q2, kj) for kj, _, _, _ in blocks] for q2, blocks, _, _ in streams]
    sps = []
    for z_list, (_, blocks, _, _) in zip(zs, streams):
        sp_list = []
        for z, (_, _, _, mask) in zip(z_list, blocks):
            sp = _softplus(z)
            sp_list.append(sp if mask is None else jnp.where(mask, sp, 0.0))
        sps.append(sp_list)
    crs = []
    for sp_list, (q2, blocks, _, _) in zip(sps, streams):
        cr_list = [None] * len(blocks)
        rows = q2.shape[0]
        for tk in sorted({b[0].shape[0] for b in blocks}):
            idx = [n for n, b in enumerate(blocks) if b[0].shape[0] == tk]
            hi, lo = _split2(jnp.concatenate([sp_list[n] for n in idx], axis=0))
            u = blocks[idx[0]][2]
            cr = _dot(hi, u) + _dot(lo, u)
            for m, n in enumerate(idx):
                cr_list[n] = cr[m * rows:(m + 1) * rows]
        crs.append(cr_list)
    ws = []
    carries = []
    for z_list, sp_list, cr_list, (_, blocks, carry, _) in zip(zs, sps, crs, streams):
        w_list = []
        for z, sp, cr, (kj, _, _, mask) in zip(z_list, sp_list, cr_list, blocks):
            tk = kj.shape[0]
            w = jnp.exp(z - sp - cr[:, :tk] - carry[:, :tk])
            if mask is not None:
                w = jnp.where(mask, w, 0.0)
            w_list.append(w.astype(BF16))
            rs = cr[:, tk:]
            if tk < LANES:
                rs = jnp.concatenate([rs] * (LANES // tk), axis=1)
            carry = carry + rs
        ws.append(w_list)
        carries.append(carry)
    out = []
    for w_list, carry, (_, blocks, _, acc) in zip(ws, carries, streams):
        pv = None
        for w, (_, vj, _, _) in zip(w_list, blocks):
            d = _dot(w, vj)
            pv = d if pv is None else pv + d
        out.append((carry, acc + pv))
    return out


def _sb_kernel(*refs, tq, n_sub, n_past_blocks):
    if n_past_blocks:
        q_ref, k_ref, v_ref, pk_ref, pv_ref, o_ref = refs
    else:
        q_ref, k_ref, v_ref, o_ref = refs
    i = pl.program_id(2)
    lane = _iota((tq, LANES), 1)
    u_own = _cumsum_matrix(tq)
    strict = _iota((2 * tq, tq), 1) < _iota((2 * tq, tq), 0) % tq
    zeros = jnp.zeros((2 * tq, LANES), F32)

    def stacked_heads(r):
        q = q_ref[r * tq:(r + 1) * tq, :]
        zero_q = jnp.zeros_like(q)
        return jnp.concatenate([jnp.where(lane < SB_HD, q, zero_q), jnp.where(lane < SB_HD, zero_q, q)], axis=0)

    q2s = [stacked_heads(r) for r in range(n_sub)]

    if n_past_blocks:
        u_past = _cumsum_matrix(LANES)
        blocks = [(k_ref[...], v_ref[...], u_own, strict)]
        for j in reversed(range(n_past_blocks)):
            rows = slice(j * LANES, (j + 1) * LANES)
            blocks.append((pk_ref[rows, :].astype(BF16), pv_ref[rows, :].astype(BF16), u_past, None))
        accs = [_sb_unit([(q2s[0], blocks, zeros, zeros)])[0][1]]
    else:
        def load_unit(r, n, first):
            blocks = []
            for kk in range(SB_UNIT):
                j = i * n_sub + r - n * SB_UNIT - kk
                rows = pl.ds(pl.multiple_of(jnp.maximum(j, 0) * tq, tq), tq)
                vj = v_ref[rows, :]
                diag = first and kk == 0
                if not diag:
                    vj = jnp.where(j >= 0, vj, jnp.zeros_like(vj))
                blocks.append((k_ref[rows, :], vj, u_own, strict if diag else None))
            return blocks

        def min_carry(state):
            m = state[0][0]
            for carry, _ in state[1:]:
                m = jnp.minimum(m, carry)
            return jnp.min(m)

        state = _sb_unit([(q2s[r], load_unit(r, 0, True), zeros, zeros) for r in range(n_sub)])
        n_units = (i * n_sub + n_sub - 1 + SB_UNIT) // SB_UNIT

        def cond(st):
            return (st[0] < n_units) & (st[1] < SB_ZERO_LOG)

        def body(st):
            n, _, state = st
            state = _sb_unit([(q2s[r], load_unit(r, n, False), *state[r]) for r in range(n_sub)])
            return n + 1, min_carry(state), state

        state = lax.while_loop(cond, body, (jnp.int32(1), min_carry(state), state))[2]
        accs = [acc for _, acc in state]
    for r, acc in enumerate(accs):
        o_ref[r * tq:(r + 1) * tq, :] = jnp.where(lane < SB_HD, acc[:tq], acc[tq:]).astype(o_ref.dtype)


def _sb(q, k, v, past_k, past_v, layer, tq, n_sub):
    bsz, t, _ = q.shape
    n_past_blocks = 0 if past_k is None else past_k.shape[2] // LANES
    qspec = pl.BlockSpec((None, tq * n_sub, LANES), lambda b, p, i: (b, i, p))
    kspec = pl.BlockSpec((None, t, LANES), lambda b, p, i: (b, 0, p))
    in_specs = [qspec, kspec, kspec]
    args = (q, k, v)
    if n_past_blocks:
        pspec = pl.BlockSpec((None, None, past_k.shape[2], LANES), lambda b, p, i: (layer, b, 0, p))
        in_specs += [pspec, pspec]
        args += (past_k, past_v)
    return pl.pallas_call(
        functools.partial(_sb_kernel, tq=tq, n_sub=n_sub, n_past_blocks=n_past_blocks),
        grid=(bsz, SB_WIDTH // LANES, t // (tq * n_sub)),
        in_specs=in_specs,
        out_specs=qspec,
        out_shape=jax.ShapeDtypeStruct((bsz, t, SB_WIDTH), BF16),
        compiler_params=pltpu.CompilerParams(
            dimension_semantics=("parallel", "parallel", "arbitrary"), vmem_limit_bytes=VMEM_LIMIT),
        name="sb",
    )(*args)


def _finish_kernel(x_ref, og_ref, gate_ref, osb_ref, gg_ref, wo_ref, g2_ref, wup_ref, wdn_ref, y_ref, *, ff_block):
    og = og_ref[...]
    normed = [_rmsnorm_rows(og[:, h * GLA_DV:(h + 1) * GLA_DV], gg_ref[...]) for h in range(GLA_HEADS)]
    gate = gate_ref[...]
    mix_g = (jnp.concatenate(normed, axis=-1) * (gate * jax.nn.sigmoid(gate))).astype(BF16)
    h = x_ref[...] + _dot(mix_g, wo_ref[:GLA_WIDTH, :]) + _dot(osb_ref[...], wo_ref[GLA_WIDTH:, :])
    hn = _rmsnorm_rows(h, g2_ref[...]).astype(BF16)
    mlp = None
    for j in range(D_FF // ff_block):
        cols = slice(j * ff_block, (j + 1) * ff_block)
        u = _dot(hn, wup_ref[:, cols])
        a = jnp.square(jnp.maximum(u, 0.0)).astype(BF16)
        d = _dot(a, wdn_ref[cols, :])
        mlp = d if mlp is None else mlp + d
    y_ref[...] = h + mlp


def _finish(x2d, og, gate, osb, pw, layer, tm):
    n = x2d.shape[0]
    row = lambda w: pl.BlockSpec((tm, w), lambda i: (i, 0))
    per_layer = lambda a: pl.BlockSpec((None,) + a.shape[1:], lambda i: (layer,) + (0,) * (a.ndim - 1))
    weights = (pw['gg'], pw['wo'], pw['g2'], pw['wup'], pw['wdn'])
    return pl.pallas_call(
        functools.partial(_finish_kernel, ff_block=1024),
        grid=(n // tm,),
        in_specs=[row(D_MODEL), row(GLA_WIDTH), row(GLA_WIDTH), row(SB_WIDTH)] + [per_layer(w) for w in weights],
        out_specs=row(D_MODEL),
        out_shape=jax.ShapeDtypeStruct((n, D_MODEL), F32),
        compiler_params=pltpu.CompilerParams(dimension_semantics=("parallel",), vmem_limit_bytes=VMEM_LIMIT),
        name="finish",
    )(x2d, og, gate, osb, *weights)


def _prepare_weights(norm1_g, w_in, w_a2, b_a2, q_norm_g, k_norm_g, gla_norm_g, w_out, norm2_g, w_up, w_down):
    gate_end = _GATE_COLS + GATE_RANK
    win = jnp.concatenate(
        [w_in[:, :, :gate_end], jnp.zeros((DEPTH, D_MODEL, LANES - GATE_RANK), w_in.dtype), w_in[:, :, gate_end:]],
        axis=2).astype(BF16)
    row = lambda a: a.reshape(DEPTH, 1, a.shape[-1])
    return {
        'g1': row(norm1_g),
        'win': win,
        'wa2': jnp.pad(w_a2, ((0, 0), (0, LANES - GATE_RANK), (0, 0))).astype(BF16),
        'ba2': row(b_a2),
        'qg': row(jnp.tile(q_norm_g, (1, SB_HEADS))),
        'kg': row(jnp.tile(k_norm_g, (1, SB_HEADS))),
        'gg': row(gla_norm_g),
        'wo': w_out.astype(BF16),
        'g2': row(norm2_g),
        'wup': w_up.astype(BF16),
        'wdn': w_down.astype(BF16),
    }


def _stream_layer(x, pw, layer, past_k, past_v, s0, tm, tf, tg, tq, n_sub):
    bsz, t, _ = x.shape
    x2d = x.reshape(bsz * t, D_MODEL)
    gq, gk, la, gv, gate, sq, skb, svb, skf, svf = _project(x2d, pw, layer, tm)
    r3 = lambda a: a.reshape(bsz, t, a.shape[-1])
    o_gla, s_new = _gla(r3(gq), r3(gk), r3(la), r3(gv), s0, layer, tg)
    o_sb = _sb(r3(sq), r3(skb), r3(svb), past_k, past_v, layer, tq, n_sub)
    y = _finish(x2d, o_gla.reshape(bsz * t, GLA_WIDTH), gate, o_sb.reshape(bsz * t, SB_WIDTH), pw, layer, tf)
    heads = lambda a: a.reshape(bsz, t, SB_HEADS, SB_HD)
    return y.reshape(bsz, t, D_MODEL), heads(skf), heads(svf), s_new


def kernel(x_prompt, x_sample, cache_sb_k, cache_sb_v, state_gla, norm1_g, w_in, w_a2, b_a2, q_norm_g, k_norm_g,
           gla_norm_g, w_out, norm2_g, w_up, w_down):
    pw = _prepare_weights(norm1_g, w_in, w_a2, b_a2, q_norm_g, k_norm_g, gla_norm_g, w_out, norm2_g, w_up, w_down)
    nb, past = cache_sb_k.shape[1], cache_sb_k.shape[2]
    past_k = cache_sb_k.reshape(DEPTH, nb, past, SB_WIDTH)
    past_v = cache_sb_v.reshape(DEPTH, nb, past, SB_WIDTH)
    xp, xs = x_prompt, x_sample
    outs = [[] for _ in range(6)]
    for layer in range(DEPTH):
        xp, kp, vp, sp = _stream_layer(xp, pw, layer, None, None, None, tm=512, tf=256, tg=512, tq=128, n_sub=4)
        xs, ks, vs, ss = _stream_layer(xs, pw, layer, past_k, past_v, state_gla, tm=512, tf=256, tg=CHUNK, tq=CHUNK,
                                       n_sub=1)
        for lst, a in zip(outs, (kp, vp, sp, ks, vs, ss)):
            lst.append(a)
    return (xp, xs) + tuple(jnp.stack(lst) for lst in outs)
```

```python
import functools

import jax
import jax.numpy as jnp
from jax import lax
from jax.experimental import pallas as pl
from jax.experimental.pallas import tpu as pltpu

D_MODEL = 1024
DEPTH = 2
CHUNK = 64
GLA_HEADS = 4
GLA_DK = 64
GLA_DV = 128
GLA_KEY = GLA_HEADS * GLA_DK
GLA_WIDTH = GLA_HEADS * GLA_DV
GATE_RANK = 16
GATE_NORM = 16.0
SB_HEADS = 8
SB_HD = 64
SB_WIDTH = SB_HEADS * SB_HD
D_FF = 4 * D_MODEL
EPS = 1e-6

LANES = 128
ROW_TILE = 16
SB_UNIT = 3
SB_ZERO_LOG = 110.0
LOG2E = 1.4426950408889634
VMEM_LIMIT = 56 * 1024 * 1024

_GATE_COLS = 2 * GLA_KEY + 2 * GLA_WIDTH
_COL_QK = (0, 2 * GLA_KEY)
_COL_GV = (2 * GLA_KEY, 2 * GLA_KEY + GLA_WIDTH)
_COL_GATE = (2 * GLA_KEY + GLA_WIDTH, _GATE_COLS)
_COL_ALR = (_GATE_COLS, _GATE_COLS + LANES)
_COL_SQ = (_GATE_COLS + LANES, _GATE_COLS + LANES + SB_WIDTH)
_COL_SK = (_COL_SQ[1], _COL_SQ[1] + SB_WIDTH)
_COL_SV = (_COL_SK[1], _COL_SK[1] + SB_WIDTH)
IN_COLS_PADDED = _COL_SV[1]

F32 = jnp.float32
BF16 = jnp.bfloat16


def _dot(a, b):
    return jnp.dot(a, b, preferred_element_type=F32)


def _dot_nt(a, b):
    return lax.dot_general(a, b, (((1,), (1,)), ((), ())), preferred_element_type=F32)


def _iota(shape, dim):
    return lax.broadcasted_iota(jnp.int32, shape, dim)


def _split3(x):
    p1 = x.astype(BF16)
    r1 = x - p1.astype(F32)
    p2 = r1.astype(BF16)
    p3 = (r1 - p2.astype(F32)).astype(BF16)
    return p1, p2, p3


def _split2(x):
    p1 = x.astype(BF16)
    p2 = (x - p1.astype(F32)).astype(BF16)
    return p1, p2


def _softplus(z):
    return jnp.maximum(z, 0.0) + jnp.log(1.0 + jnp.exp(-jnp.abs(z)))


def _rmsnorm_rows(x, g):
    ms = jnp.mean(x * x, axis=-1, keepdims=True)
    return x * lax.rsqrt(ms + EPS) * g


def _project_kernel(*refs, aliased):
    x_ref, g1_ref, win_ref, wa2_ref, ba2_ref, qg_ref, kg_ref = refs[:7]
    gq_ref, gk_ref, la_ref, gv_ref, gate_ref, sq_ref, skb_ref, svb_ref, skf_ref, svf_ref = refs[7 + 2 * aliased:]
    xn = _rmsnorm_rows(x_ref[...], g1_ref[...]).astype(BF16)
    proj = lambda cols: _dot(xn, win_ref[:, cols[0]:cols[1]])

    qk = proj(_COL_QK)
    gv = proj(_COL_GV)
    gate = proj(_COL_GATE)
    alr = proj(_COL_ALR)
    sq = proj(_COL_SQ)
    sk = proj(_COL_SK)
    sv = proj(_COL_SV)
    y = _dot(alr.astype(BF16), wa2_ref[...]) + ba2_ref[...]

    r = _iota((LANES, LANES), 0) // SB_HD
    c = _iota((LANES, LANES), 1) // SB_HD
    ones_bd = jnp.where(r == c, 1.0, 0.0).astype(BF16)

    def head_mean_squares(s):
        out = []
        for j in range(SB_WIDTH // LANES):
            sj = s[:, j * LANES:(j + 1) * LANES]
            hi, lo = _split2(sj * sj)
            out.append((_dot(hi, ones_bd) + _dot(lo, ones_bd)) * (1.0 / SB_HD))
        return jnp.concatenate(out, axis=-1)

    ms_q = head_mean_squares(sq)
    ms_k = head_mean_squares(sk)

    gq_ref[...] = qk[:, :GLA_KEY] * (GLA_DK ** -0.5)
    gk_ref[...] = qk[:, GLA_KEY:]
    gv_ref[...] = gv.astype(BF16)
    gate_ref[...] = gate
    svb_ref[...] = sv.astype(BF16)
    svf_ref[...] = sv
    la_ref[...] = (jnp.minimum(y, 0.0) - jnp.log(1.0 + jnp.exp(-jnp.abs(y)))) * (1.0 / GATE_NORM)
    sq_ref[...] = (sq * lax.rsqrt(ms_q + EPS) * qg_ref[...] * (SB_HD ** -0.5)).astype(BF16)
    sk_n = sk * lax.rsqrt(ms_k + EPS) * kg_ref[...]
    skb_ref[...] = sk_n.astype(BF16)
    skf_ref[...] = sk_n


def _project(x2d, pw, layer, kv_bufs, tm):
    n = x2d.shape[0]
    aliased = kv_bufs is not None
    row = lambda w: pl.BlockSpec((tm, w), lambda i: (i, 0))
    per_layer = lambda a: pl.BlockSpec((None,) + a.shape[1:], lambda i: (layer,) + (0,) * (a.ndim - 1))
    weights = (pw['g1'], pw['win'], pw['wa2'], pw['ba2'], pw['qg'], pw['kg'])
    out_cols = ((GLA_KEY, F32), (GLA_KEY, F32), (GLA_KEY, F32), (GLA_WIDTH, BF16), (GLA_WIDTH, F32),
                (SB_WIDTH, BF16), (SB_WIDTH, BF16), (SB_WIDTH, BF16))
    kv_spec = pl.BlockSpec((None, tm, SB_WIDTH), lambda i: (layer, i, 0))
    kv_shape = jax.ShapeDtypeStruct((DEPTH, n, SB_WIDTH), F32)
    n_in = 1 + len(weights)
    return pl.pallas_call(
        functools.partial(_project_kernel, aliased=aliased),
        grid=(n // tm,),
        in_specs=[row(D_MODEL)] + [per_layer(w) for w in weights]
        + ([pl.BlockSpec(memory_space=pl.ANY)] * 2 if aliased else []),
        out_specs=[row(w) for w, _ in out_cols] + [kv_spec, kv_spec],
        out_shape=[jax.ShapeDtypeStruct((n, w), dt) for w, dt in out_cols] + [kv_shape, kv_shape],
        input_output_aliases={n_in: len(out_cols), n_in + 1: len(out_cols) + 1} if aliased else {},
        compiler_params=pltpu.CompilerParams(dimension_semantics=("parallel",), vmem_limit_bytes=VMEM_LIMIT),
        name="project",
    )(x2d, *weights, *(kv_bufs if aliased else ()))


def _gla_kernel(*refs, n_chunks, chunks_per_step, has_init):
    if has_init:
        q_ref, k_ref, la_ref, v_ref, s0_ref, o_ref, sout_ref, s_scr, b_scr = refs
    else:
        q_ref, k_ref, la_ref, v_ref, o_ref, sout_ref, s_scr, b_scr = refs
    g = pl.program_id(2)

    @pl.when(g == 0)
    def _():
        if has_init:
            s_scr[...] = s0_ref[...].reshape(2 * GLA_DK, GLA_DV)
        else:
            s_scr[...] = jnp.zeros_like(s_scr)

    rr = _iota((CHUNK, CHUNK), 0)
    cc = _iota((CHUNK, CHUNK), 1)
    tri = jnp.where(cc <= rr, 1.0, 0.0).astype(BF16)
    r = _iota((2 * LANES, 2 * LANES), 0) // GLA_DK
    c = _iota((2 * LANES, 2 * LANES), 1) // GLA_DK
    ones_bd2 = jnp.where(r == c, 1.0, 0.0).astype(BF16)
    lane = _iota((CHUNK, LANES), 1)
    lane_t = lane % CHUNK
    tile_t = _iota((ROW_TILE, LANES), 1) % CHUNK
    srow = _iota((CHUNK, LANES), 0)
    head_a = lane < GLA_DK

    n_sub = CHUNK // ROW_TILE
    sub = lambda g: slice(g * ROW_TILE, (g + 1) * ROW_TILE)

    def operands(slot, base, b):
        b2 = b * LOG2E
        b_scr[slot] = b2
        rows = pl.ds(base, CHUNK)
        q = q_ref[rows, :]
        k = k_ref[rows, :]

        b_end = jnp.concatenate(
            [jnp.broadcast_to(b[(g + 1) * ROW_TILE - 1:(g + 1) * ROW_TILE], (ROW_TILE, LANES)) for g in range(n_sub)],
            axis=0)
        k_dec = (k * jnp.exp(b_end - b)).astype(BF16)
        q_dec_rhs = []
        for g in range(n_sub - 1):
            lo = (g + 1) * ROW_TILE
            q_dec = q[lo:] * jnp.exp(b[lo:] - b[lo - 1:lo])
            pad = jnp.zeros((lo, LANES), F32)
            in_a = _iota((CHUNK - lo, LANES), 1) < GLA_DK
            q_dec_rhs.append(
                jnp.concatenate([pad, jnp.where(in_a, q_dec, 0.0), pad, jnp.where(in_a, 0.0, q_dec)],
                                axis=0).astype(BF16))

        es = []
        for t in range(CHUNK):
            g = t // ROW_TILE
            brow = b_scr[slot, t:t + 1, :]
            qrow = q_ref[pl.ds(base + t, 1), :]
            es.append(jnp.exp2(brow - b2[sub(g)]).astype(BF16) * (qrow * k[sub(g)]).astype(BF16))
        diag_lhs = jnp.concatenate(
            [jnp.concatenate([es[2 * p], es[2 * p + 1]], axis=1) for p in range(CHUNK // 2)], axis=0)

        qe = q * jnp.exp(b)
        qe_a = jnp.where(head_a, qe, 0.0).astype(BF16)
        qe_b = jnp.where(head_a, 0.0, qe).astype(BF16)
        b_t = b.T
        b_last = b_t[:, CHUNK - 1:CHUNK]
        kd_t = (k.T * jnp.exp(b_last - b_t)).astype(BF16)
        return k_dec, q_dec_rhs, diag_lhs, qe_a, qe_b, kd_t, jnp.exp(b_last)

    def scores(off_tiles, red):
        pt_tiles = list(off_tiles) + [jnp.zeros((ROW_TILE, LANES), F32)]
        for t in range(CHUNK):
            g, p, half = t // ROW_TILE, t // 2, t % 2
            r_t = red[p * ROW_TILE:(p + 1) * ROW_TILE, half * LANES:(half + 1) * LANES]
            pt_tiles[g] = jnp.where(tile_t == t, r_t, pt_tiles[g])
        pt = jnp.concatenate(pt_tiles, axis=0)
        return jnp.where(srow <= lane_t, pt, 0.0)

    def step_body(si, carry):
        slots = range(chunks_per_step)
        bases = [pl.multiple_of((si * chunks_per_step + slot) * CHUNK, CHUNK) for slot in slots]
        bs = []
        for base in bases:
            p1, p2, p3 = _split3(la_ref[pl.ds(base, CHUNK), :])
            bs.append(_dot(tri, p1) + _dot(tri, p2) + _dot(tri, p3))
        ops = [operands(slot, bases[slot], bs[slot]) for slot in slots]
        v2s = [v_ref[pl.ds(base, CHUNK), :] for base in bases]
        offs = [[_dot_nt(op[0][sub(g)], op[1][g]) for g in range(n_sub - 1)] for op in ops]
        reds = [_dot(op[2], ones_bd2) for op in ops]
        upds = [_dot(op[5], v2) for op, v2 in zip(ops, v2s)]
        pts = [scores(off, red).T.astype(BF16) for off, red in zip(offs, reds)]
        o_intras = [_dot(pt, v2) for pt, v2 in zip(pts, v2s)]

        s = s_scr[...]
        for slot in slots:
            _, _, _, qe_a, qe_b, _, decay = ops[slot]
            rows = pl.ds(bases[slot], CHUNK)
            s_bf = s.astype(BF16)
            o_ref[rows, :GLA_DV] = o_intras[slot][:CHUNK, :GLA_DV] + _dot(qe_a, s_bf)
            o_ref[rows, GLA_DV:] = o_intras[slot][CHUNK:, GLA_DV:] + _dot(qe_b, s_bf)
            upd = upds[slot]
            s = decay * s + jnp.concatenate([upd[:GLA_DK, :GLA_DV], upd[GLA_DK:, GLA_DV:]], axis=0)
        s_scr[...] = s
        return carry

    lax.fori_loop(0, n_chunks // chunks_per_step, step_body, 0)

    @pl.when(g == pl.num_programs(2) - 1)
    def _():
        sout_ref[...] = s_scr[...].reshape(2, GLA_DK, GLA_DV)


def _gla(gq, gk, la, gv, s0, layer, tg):
    bsz, t, _ = gq.shape
    has_init = s0 is not None
    n_chunks = tg // CHUNK
    chunks_per_step = 8 if n_chunks % 8 == 0 else 1
    qspec = pl.BlockSpec((None, tg, LANES), lambda b, p, g: (b, g, p))
    vspec = pl.BlockSpec((None, tg, 2 * GLA_DV), lambda b, p, g: (b, g, p))
    sspec = pl.BlockSpec((None, 2, GLA_DK, GLA_DV), lambda b, p, g: (b, p, 0, 0))
    s0spec = pl.BlockSpec((None, None, 2, GLA_DK, GLA_DV), lambda b, p, g: (layer, b, p, 0, 0))
    in_specs = [qspec, qspec, qspec, vspec] + ([s0spec] if has_init else [])
    args = (gq, gk, la, gv) + ((s0,) if has_init else ())
    return pl.pallas_call(
        functools.partial(_gla_kernel, n_chunks=n_chunks, chunks_per_step=chunks_per_step, has_init=has_init),
        grid=(bsz, GLA_HEADS // 2, t // tg),
        in_specs=in_specs,
        out_specs=[vspec, sspec],
        out_shape=[jax.ShapeDtypeStruct((bsz, t, GLA_WIDTH), F32),
                   jax.ShapeDtypeStruct((bsz, GLA_HEADS, GLA_DK, GLA_DV), F32)],
        scratch_shapes=[pltpu.VMEM((2 * GLA_DK, GLA_DV), F32), pltpu.VMEM((chunks_per_step, CHUNK, LANES), F32)],
        compiler_params=pltpu.CompilerParams(
            dimension_semantics=("parallel", "parallel", "arbitrary"), vmem_limit_bytes=VMEM_LIMIT),
        name="gla",
    )(*args)


def _cumsum_matrix(tk):
    r = _iota((tk, 2 * tk), 0)
    c = _iota((tk, 2 * tk), 1)
    return jnp.where((c >= tk) | (r > c), 1.0, 0.0).astype(BF16)


def _sb_unit(streams):
    zs = [[_dot_nt(q2, kj) for kj, _, _, _ in blocks] for q2, blocks, _, _ in streams]
    sps = []
    for z_list, (_, blocks, _, _) in zip(zs, streams):
        sp_list = []
        for z, (_, _, _, mask) in zip(z_list, blocks):
            sp = _softplus(z)
            sp_list.append(sp if mask is None else jnp.where(mask, sp, 0.0))
        sps.append(sp_list)
    crs = []
    for sp_list, (q2, blocks, _, _) in zip(sps, streams):
        cr_list = [None] * len(blocks)
        rows = q2.shape[0]
        for tk in sorted({b[0].shape[0] for b in blocks}):
            idx = [n for n, b in enumerate(blocks) if b[0].shape[0] == tk]
            u = blocks[idx[0]][2]
            cr = _dot(jnp.concatenate([sp_list[n] for n in idx], axis=0).astype(BF16), u)
            for m, n in enumerate(idx):
                cr_list[n] = cr[m * rows:(m + 1) * rows]
        crs.append(cr_list)
    ws = []
    carries = []
    for z_list, sp_list, cr_list, (_, blocks, carry, _) in zip(zs, sps, crs, streams):
        w_list = []
        for z, sp, cr, (kj, _, _, mask) in zip(z_list, sp_list, cr_list, blocks):
            tk = kj.shape[0]
            w = jnp.exp(z - sp - cr[:, :tk] - carry[:, :tk])
            if mask is not None:
                w = jnp.where(mask, w, 0.0)
            w_list.append(w.astype(BF16))
            rs = cr[:, tk:]
            if tk < LANES:
                rs = jnp.concatenate([rs] * (LANES // tk), axis=1)
            carry = carry + rs
        ws.append(w_list)
        carries.append(carry)
    out = []
    for w_list, carry, (_, blocks, _, acc) in zip(ws, carries, streams):
        pv = None
        for w, (_, vj, _, _) in zip(w_list, blocks):
            d = _dot(w, vj)
            pv = d if pv is None else pv + d
        out.append((carry, acc + pv))
    return out


def _sb_kernel(*refs, tq, n_sub, n_past_blocks):
    if n_past_blocks:
        q_ref, k_ref, v_ref, pk_ref, pv_ref, o_ref = refs
    else:
        q_ref, k_ref, v_ref, o_ref = refs
    i = pl.program_id(2)
    lane = _iota((tq, LANES), 1)
    u_own = _cumsum_matrix(tq)
    strict = _iota((2 * tq, tq), 1) < _iota((2 * tq, tq), 0) % tq
    zeros = jnp.zeros((2 * tq, LANES), F32)

    def stacked_heads(r):
        q = q_ref[r * tq:(r + 1) * tq, :]
        zero_q = jnp.zeros_like(q)
        return jnp.concatenate([jnp.where(lane < SB_HD, q, zero_q), jnp.where(lane < SB_HD, zero_q, q)], axis=0)

    q2s = [stacked_heads(r) for r in range(n_sub)]

    if n_past_blocks:
        u_past = _cumsum_matrix(LANES)
        blocks = [(k_ref[...], v_ref[...], u_own, strict)]
        for j in reversed(range(n_past_blocks)):
            rows = slice(j * LANES, (j + 1) * LANES)
            blocks.append((pk_ref[rows, :].astype(BF16), pv_ref[rows, :].astype(BF16), u_past, None))
        accs = [_sb_unit([(q2s[0], blocks, zeros, zeros)])[0][1]]
    else:
        def load_unit(r, n, first):
            blocks = []
            for kk in range(SB_UNIT):
                j = i * n_sub + r - n * SB_UNIT - kk
                rows = pl.ds(pl.multiple_of(jnp.maximum(j, 0) * tq, tq), tq)
                vj = v_ref[rows, :]
                diag = first and kk == 0
                if not diag:
                    vj = jnp.where(j >= 0, vj, jnp.zeros_like(vj))
                blocks.append((k_ref[rows, :], vj, u_own, strict if diag else None))
            return blocks

        def min_carry(state):
            m = state[0][0]
            for carry, _ in state[1:]:
                m = jnp.minimum(m, carry)
            return jnp.min(m)

        state = _sb_unit([(q2s[r], load_unit(r, 0, True), zeros, zeros) for r in range(n_sub)])
        n_units = (i * n_sub + n_sub - 1 + SB_UNIT) // SB_UNIT

        def cond(st):
            return (st[0] < n_units) & (st[1] < SB_ZERO_LOG)

        def body(st):
            n, _, state = st
            state = _sb_unit([(q2s[r], load_unit(r, n, False), *state[r]) for r in range(n_sub)])
            return n + 1, min_carry(state), state

        state = lax.while_loop(cond, body, (jnp.int32(1), min_carry(state), state))[2]
        accs = [acc for _, acc in state]
    for r, acc in enumerate(accs):
        o_ref[r * tq:(r + 1) * tq, :] = jnp.where(lane < SB_HD, acc[:tq], acc[tq:]).astype(o_ref.dtype)


def _sb(q, k, v, past_k, past_v, layer, tq, n_sub):
    bsz, t, _ = q.shape
    n_past_blocks = 0 if past_k is None else past_k.shape[2] // LANES
    qspec = pl.BlockSpec((None, tq * n_sub, LANES), lambda b, p, i: (b, i, p))
    kspec = pl.BlockSpec((None, t, LANES), lambda b, p, i: (b, 0, p))
    in_specs = [qspec, kspec, kspec]
    args = (q, k, v)
    if n_past_blocks:
        pspec = pl.BlockSpec((None, None, past_k.shape[2], LANES), lambda b, p, i: (layer, b, 0, p))
        in_specs += [pspec, pspec]
        args += (past_k, past_v)
    return pl.pallas_call(
        functools.partial(_sb_kernel, tq=tq, n_sub=n_sub, n_past_blocks=n_past_blocks),
        grid=(bsz, SB_WIDTH // LANES, t // (tq * n_sub)),
        in_specs=in_specs,
        out_specs=qspec,
        out_shape=jax.ShapeDtypeStruct((bsz, t, SB_WIDTH), BF16),
        compiler_params=pltpu.CompilerParams(
            dimension_semantics=("parallel", "parallel", "arbitrary"), vmem_limit_bytes=VMEM_LIMIT),
        name="sb",
    )(*args)


def _finish_kernel(x_ref, og_ref, gate_ref, osb_ref, gg_ref, wo_ref, g2_ref, wup_ref, wdn_ref, y_ref, *, ff_block):
    og = og_ref[...]
    normed = [_rmsnorm_rows(og[:, h * GLA_DV:(h + 1) * GLA_DV], gg_ref[...]) for h in range(GLA_HEADS)]
    gate = gate_ref[...]
    mix_g = (jnp.concatenate(normed, axis=-1) * (gate * jax.nn.sigmoid(gate))).astype(BF16)
    h = x_ref[...] + _dot(mix_g, wo_ref[:GLA_WIDTH, :]) + _dot(osb_ref[...], wo_ref[GLA_WIDTH:, :])
    hn = _rmsnorm_rows(h, g2_ref[...]).astype(BF16)
    mlp = None
    for j in range(D_FF // ff_block):
        cols = slice(j * ff_block, (j + 1) * ff_block)
        u = _dot(hn, wup_ref[:, cols])
        a = jnp.square(jnp.maximum(u, 0.0)).astype(BF16)
        d = _dot(a, wdn_ref[cols, :])
        mlp = d if mlp is None else mlp + d
    y_ref[...] = h + mlp


def _finish(x2d, og, gate, osb, pw, layer, tm):
    n = x2d.shape[0]
    row = lambda w: pl.BlockSpec((tm, w), lambda i: (i, 0))
    per_layer = lambda a: pl.BlockSpec((None,) + a.shape[1:], lambda i: (layer,) + (0,) * (a.ndim - 1))
    weights = (pw['gg'], pw['wo'], pw['g2'], pw['wup'], pw['wdn'])
    return pl.pallas_call(
        functools.partial(_finish_kernel, ff_block=1024),
        grid=(n // tm,),
        in_specs=[row(D_MODEL), row(GLA_WIDTH), row(GLA_WIDTH), row(SB_WIDTH)] + [per_layer(w) for w in weights],
        out_specs=row(D_MODEL),
        out_shape=jax.ShapeDtypeStruct((n, D_MODEL), F32),
        compiler_params=pltpu.CompilerParams(dimension_semantics=("parallel",), vmem_limit_bytes=VMEM_LIMIT),
        name="finish",
    )(x2d, og, gate, osb, *weights)


def _prepare_weights(norm1_g, w_in, w_a2, b_a2, q_norm_g, k_norm_g, gla_norm_g, w_out, norm2_g, w_up, w_down):
    gate_end = _GATE_COLS + GATE_RANK
    win = jnp.concatenate(
        [w_in[:, :, :gate_end], jnp.zeros((DEPTH, D_MODEL, LANES - GATE_RANK), w_in.dtype), w_in[:, :, gate_end:]],
        axis=2).astype(BF16)
    row = lambda a: a.reshape(DEPTH, 1, a.shape[-1])
    return {
        'g1': row(norm1_g),
        'win': win,
        'wa2': jnp.pad(w_a2, ((0, 0), (0, LANES - GATE_RANK), (0, 0))).astype(BF16),
        'ba2': row(b_a2),
        'qg': row(jnp.tile(q_norm_g, (1, SB_HEADS))),
        'kg': row(jnp.tile(k_norm_g, (1, SB_HEADS))),
        'gg': row(gla_norm_g),
        'wo': w_out.astype(BF16),
        'g2': row(norm2_g),
        'wup': w_up.astype(BF16),
        'wdn': w_down.astype(BF16),
    }


def _stream_layer(x, pw, layer, kv_bufs, past_k, past_v, s0, tm, tf, tg, tq, n_sub):
    bsz, t, _ = x.shape
    x2d = x.reshape(bsz * t, D_MODEL)
    gq, gk, la, gv, gate, sq, skb, svb, skf, svf = _project(x2d, pw, layer, kv_bufs, tm)
    r3 = lambda a: a.reshape(bsz, t, a.shape[-1])
    o_gla, s_new = _gla(r3(gq), r3(gk), r3(la), r3(gv), s0, layer, tg)
    o_sb = _sb(r3(sq), r3(skb), r3(svb), past_k, past_v, layer, tq, n_sub)
    y = _finish(x2d, o_gla.reshape(bsz * t, GLA_WIDTH), gate, o_sb.reshape(bsz * t, SB_WIDTH), pw, layer, tf)
    return y.reshape(bsz, t, D_MODEL), (skf, svf), s_new


def kernel(x_prompt, x_sample, cache_sb_k, cache_sb_v, state_gla, norm1_g, w_in, w_a2, b_a2, q_norm_g, k_norm_g,
           gla_norm_g, w_out, norm2_g, w_up, w_down):
    pw = _prepare_weights(norm1_g, w_in, w_a2, b_a2, q_norm_g, k_norm_g, gla_norm_g, w_out, norm2_g, w_up, w_down)
    nb, past = cache_sb_k.shape[1], cache_sb_k.shape[2]
    past_k = cache_sb_k.reshape(DEPTH, nb, past, SB_WIDTH)
    past_v = cache_sb_v.reshape(DEPTH, nb, past, SB_WIDTH)
    xp, xs = x_prompt, x_sample
    kv_p, kv_s, states_p, states_s = None, None, [], []
    for layer in range(DEPTH):
        xp, kv_p, sp = _stream_layer(xp, pw, layer, kv_p, None, None, None, tm=512, tf=256, tg=512, tq=128, n_sub=4)
        xs, kv_s, ss = _stream_layer(xs, pw, layer, kv_s, past_k, past_v, state_gla, tm=512, tf=256, tg=CHUNK,
                                     tq=CHUNK, n_sub=1)
        states_p.append(sp)
        states_s.append(ss)
    cache = lambda a, x: a.reshape((DEPTH,) + x.shape[:2] + (SB_HEADS, SB_HD))
    return (xp, xs, cache(kv_p[0], xp), cache(kv_p[1], xp), jnp.stack(states_p),
            cache(kv_s[0], xs), cache(kv_s[1], xs), jnp.stack(states_s))
```

```python
import functools

import jax
import jax.numpy as jnp
from jax import lax
from jax.experimental import pallas as pl
from jax.experimental.pallas import tpu as pltpu

D_MODEL = 1024
DEPTH = 2
CHUNK = 64
GLA_HEADS = 4
GLA_DK = 64
GLA_DV = 128
GLA_KEY = GLA_HEADS * GLA_DK
GLA_WIDTH = GLA_HEADS * GLA_DV
GATE_RANK = 16
GATE_NORM = 16.0
SB_HEADS = 8
SB_HD = 64
SB_WIDTH = SB_HEADS * SB_HD
D_FF = 4 * D_MODEL
EPS = 1e-6

LANES = 128
ROW_TILE = 16
SB_UNIT = 3
SB_ZERO_LOG = 110.0
LOG2E = 1.4426950408889634
VMEM_LIMIT = 56 * 1024 * 1024

_GATE_COLS = 2 * GLA_KEY + 2 * GLA_WIDTH
_COL_QK = (0, 2 * GLA_KEY)
_COL_GV = (2 * GLA_KEY, 2 * GLA_KEY + GLA_WIDTH)
_COL_GATE = (2 * GLA_KEY + GLA_WIDTH, _GATE_COLS)
_COL_ALR = (_GATE_COLS, _GATE_COLS + LANES)
_COL_SQ = (_GATE_COLS + LANES, _GATE_COLS + LANES + SB_WIDTH)
_COL_SK = (_COL_SQ[1], _COL_SQ[1] + SB_WIDTH)
_COL_SV = (_COL_SK[1], _COL_SK[1] + SB_WIDTH)
IN_COLS_PADDED = _COL_SV[1]

F32 = jnp.float32
BF16 = jnp.bfloat16


def _dot(a, b):
    return jnp.dot(a, b, preferred_element_type=F32)


def _dot_nt(a, b):
    return lax.dot_general(a, b, (((1,), (1,)), ((), ())), preferred_element_type=F32)


def _iota(shape, dim):
    return lax.broadcasted_iota(jnp.int32, shape, dim)


def _split3(x):
    p1 = x.astype(BF16)
    r1 = x - p1.astype(F32)
    p2 = r1.astype(BF16)
    p3 = (r1 - p2.astype(F32)).astype(BF16)
    return p1, p2, p3


def _split2(x):
    p1 = x.astype(BF16)
    p2 = (x - p1.astype(F32)).astype(BF16)
    return p1, p2


def _softplus(z):
    return jnp.maximum(z, 0.0) + jnp.log(1.0 + jnp.exp(-jnp.abs(z)))


def _rmsnorm_rows(x, g):
    ms = jnp.mean(x * x, axis=-1, keepdims=True)
    return x * lax.rsqrt(ms + EPS) * g


def _project_kernel(*refs, aliased):
    x_ref, g1_ref, win_ref, wa2_ref, ba2_ref, qg_ref, kg_ref = refs[:7]
    gq_ref, gk_ref, la_ref, gv_ref, gate_ref, sq_ref, skb_ref, svb_ref, k4_ref, v4_ref = refs[7 + 2 * aliased:]
    xn = _rmsnorm_rows(x_ref[...], g1_ref[...]).astype(BF16)
    proj = lambda cols: _dot(xn, win_ref[:, cols[0]:cols[1]])

    qk = proj(_COL_QK)
    gv = proj(_COL_GV)
    gate = proj(_COL_GATE)
    alr = proj(_COL_ALR)
    sq = proj(_COL_SQ)
    sk = proj(_COL_SK)
    sv = proj(_COL_SV)
    y = _dot(alr.astype(BF16), wa2_ref[...]) + ba2_ref[...]

    r = _iota((LANES, LANES), 0) // SB_HD
    c = _iota((LANES, LANES), 1) // SB_HD
    ones_bd = jnp.where(r == c, 1.0, 0.0).astype(BF16)

    def head_mean_squares(s):
        out = []
        for j in range(SB_WIDTH // LANES):
            sj = s[:, j * LANES:(j + 1) * LANES]
            hi, lo = _split2(sj * sj)
            out.append((_dot(hi, ones_bd) + _dot(lo, ones_bd)) * (1.0 / SB_HD))
        return jnp.concatenate(out, axis=-1)

    def store_cache_layout(ref, a):
        for h in range(SB_HEADS):
            ref[pl.ds(h, a.shape[0], stride=SB_HEADS), :] = a[:, h * SB_HD:(h + 1) * SB_HD]

    ms_q = head_mean_squares(sq)
    ms_k = head_mean_squares(sk)

    gq_ref[...] = qk[:, :GLA_KEY] * (GLA_DK ** -0.5)
    gk_ref[...] = qk[:, GLA_KEY:]
    gv_ref[...] = gv.astype(BF16)
    gate_ref[...] = gate
    svb_ref[...] = sv.astype(BF16)
    store_cache_layout(v4_ref, sv)
    la_ref[...] = (jnp.minimum(y, 0.0) - jnp.log(1.0 + jnp.exp(-jnp.abs(y)))) * (1.0 / GATE_NORM)
    sq_ref[...] = (sq * lax.rsqrt(ms_q + EPS) * qg_ref[...] * (SB_HD ** -0.5)).astype(BF16)
    sk_n = sk * lax.rsqrt(ms_k + EPS) * kg_ref[...]
    skb_ref[...] = sk_n.astype(BF16)
    store_cache_layout(k4_ref, sk_n)


def _project(x2d, pw, layer, kv_bufs, tm):
    n = x2d.shape[0]
    aliased = kv_bufs is not None
    row = lambda w: pl.BlockSpec((tm, w), lambda i: (i, 0))
    per_layer = lambda a: pl.BlockSpec((None,) + a.shape[1:], lambda i: (layer,) + (0,) * (a.ndim - 1))
    weights = (pw['g1'], pw['win'], pw['wa2'], pw['ba2'], pw['qg'], pw['kg'])
    out_cols = ((GLA_KEY, F32), (GLA_KEY, F32), (GLA_KEY, F32), (GLA_WIDTH, BF16), (GLA_WIDTH, F32),
                (SB_WIDTH, BF16), (SB_WIDTH, BF16), (SB_WIDTH, BF16))
    kv_spec = pl.BlockSpec((None, tm * SB_HEADS, SB_HD), lambda i: (layer, i, 0))
    kv_shape = jax.ShapeDtypeStruct((DEPTH, n * SB_HEADS, SB_HD), F32)
    n_in = 1 + len(weights)
    return pl.pallas_call(
        functools.partial(_project_kernel, aliased=aliased),
        grid=(n // tm,),
        in_specs=[row(D_MODEL)] + [per_layer(w) for w in weights]
        + ([pl.BlockSpec(memory_space=pl.ANY)] * 2 if aliased else []),
        out_specs=[row(w) for w, _ in out_cols] + [kv_spec, kv_spec],
        out_shape=[jax.ShapeDtypeStruct((n, w), dt) for w, dt in out_cols] + [kv_shape, kv_shape],
        input_output_aliases={n_in: len(out_cols), n_in + 1: len(out_cols) + 1} if aliased else {},
        compiler_params=pltpu.CompilerParams(dimension_semantics=("parallel",), vmem_limit_bytes=VMEM_LIMIT),
        name="project",
    )(x2d, *weights, *(kv_bufs if aliased else ()))


def _gla_kernel(*refs, n_chunks, chunks_per_step, has_init):
    if has_init:
        q_ref, k_ref, la_ref, v_ref, s0_ref, o_ref, sout_ref, s_scr, b_scr = refs
    else:
        q_ref, k_ref, la_ref, v_ref, o_ref, sout_ref, s_scr, b_scr = refs
    g = pl.program_id(2)

    @pl.when(g == 0)
    def _():
        if has_init:
            s_scr[...] = s0_ref[...].reshape(2 * GLA_DK, GLA_DV)
        else:
            s_scr[...] = jnp.zeros_like(s_scr)

    rr = _iota((CHUNK, CHUNK), 0)
    cc = _iota((CHUNK, CHUNK), 1)
    tri = jnp.where(cc <= rr, 1.0, 0.0).astype(BF16)
    r = _iota((2 * LANES, 2 * LANES), 0) // GLA_DK
    c = _iota((2 * LANES, 2 * LANES), 1) // GLA_DK
    ones_bd2 = jnp.where(r == c, 1.0, 0.0).astype(BF16)
    lane = _iota((CHUNK, LANES), 1)
    lane_t = lane % CHUNK
    tile_t = _iota((ROW_TILE, LANES), 1) % CHUNK
    srow = _iota((CHUNK, LANES), 0)
    head_a = lane < GLA_DK

    n_sub = CHUNK // ROW_TILE
    sub = lambda g: slice(g * ROW_TILE, (g + 1) * ROW_TILE)

    def operands(slot, base, b):
        b2 = b * LOG2E
        b_scr[slot] = b2
        rows = pl.ds(base, CHUNK)
        q = q_ref[rows, :]
        k = k_ref[rows, :]

        b_end = jnp.concatenate(
            [jnp.broadcast_to(b[(g + 1) * ROW_TILE - 1:(g + 1) * ROW_TILE], (ROW_TILE, LANES)) for g in range(n_sub)],
            axis=0)
        k_dec = (k * jnp.exp(b_end - b)).astype(BF16)
        q_dec_rhs = []
        for g in range(n_sub - 1):
            lo = (g + 1) * ROW_TILE
            q_dec = q[lo:] * jnp.exp(b[lo:] - b[lo - 1:lo])
            pad = jnp.zeros((lo, LANES), F32)
            in_a = _iota((CHUNK - lo, LANES), 1) < GLA_DK
            q_dec_rhs.append(
                jnp.concatenate([pad, jnp.where(in_a, q_dec, 0.0), pad, jnp.where(in_a, 0.0, q_dec)],
                                axis=0).astype(BF16))

        es = []
        for t in range(CHUNK):
            g = t // ROW_TILE
            brow = b_scr[slot, t:t + 1, :]
            qrow = q_ref[pl.ds(base + t, 1), :]
            es.append(jnp.exp2(brow - b2[sub(g)]).astype(BF16) * (qrow * k[sub(g)]).astype(BF16))
        diag_lhs = jnp.concatenate(
            [jnp.concatenate([es[2 * p], es[2 * p + 1]], axis=1) for p in range(CHUNK // 2)], axis=0)

        qe = q * jnp.exp(b)
        qe_a = jnp.where(head_a, qe, 0.0).astype(BF16)
        qe_b = jnp.where(head_a, 0.0, qe).astype(BF16)
        b_t = b.T
        b_last = b_t[:, CHUNK - 1:CHUNK]
        kd_t = (k.T * jnp.exp(b_last - b_t)).astype(BF16)
        return k_dec, q_dec_rhs, diag_lhs, qe_a, qe_b, kd_t, jnp.exp(b_last)

    def scores(off_tiles, red):
        pt_tiles = list(off_tiles) + [jnp.zeros((ROW_TILE, LANES), F32)]
        for t in range(CHUNK):
            g, p, half = t // ROW_TILE, t // 2, t % 2
            r_t = red[p * ROW_TILE:(p + 1) * ROW_TILE, half * LANES:(half + 1) * LANES]
            pt_tiles[g] = jnp.where(tile_t == t, r_t, pt_tiles[g])
        pt = jnp.concatenate(pt_tiles, axis=0)
        return jnp.where(srow <= lane_t, pt, 0.0)

    def step_body(si, carry):
        slots = range(chunks_per_step)
        bases = [pl.multiple_of((si * chunks_per_step + slot) * CHUNK, CHUNK) for slot in slots]
        bs = []
        for base in bases:
            p1, p2, p3 = _split3(la_ref[pl.ds(base, CHUNK), :])
            bs.append(_dot(tri, p1) + _dot(tri, p2) + _dot(tri, p3))
        ops = [operands(slot, bases[slot], bs[slot]) for slot in slots]
        v2s = [v_ref[pl.ds(base, CHUNK), :] for base in bases]
        offs = [[_dot_nt(op[0][sub(g)], op[1][g]) for g in range(n_sub - 1)] for op in ops]
        reds = [_dot(op[2], ones_bd2) for op in ops]
        upds = [_dot(op[5], v2) for op, v2 in zip(ops, v2s)]
        pts = [scores(off, red).T.astype(BF16) for off, red in zip(offs, reds)]
        o_intras = [_dot(pt, v2) for pt, v2 in zip(pts, v2s)]

        s = s_scr[...]
        for slot in slots:
            _, _, _, qe_a, qe_b, _, decay = ops[slot]
            rows = pl.ds(bases[slot], CHUNK)
            s_bf = s.astype(BF16)
            o_ref[rows, :GLA_DV] = o_intras[slot][:CHUNK, :GLA_DV] + _dot(qe_a, s_bf)
            o_ref[rows, GLA_DV:] = o_intras[slot][CHUNK:, GLA_DV:] + _dot(qe_b, s_bf)
            upd = upds[slot]
            s = decay * s + jnp.concatenate([upd[:GLA_DK, :GLA_DV], upd[GLA_DK:, GLA_DV:]], axis=0)
        s_scr[...] = s
        return carry

    lax.fori_loop(0, n_chunks // chunks_per_step, step_body, 0)

    @pl.when(g == pl.num_programs(2) - 1)
    def _():
        sout_ref[...] = s_scr[...].reshape(2, GLA_DK, GLA_DV)


def _gla(gq, gk, la, gv, s0, layer, tg):
    bsz, t, _ = gq.shape
    has_init = s0 is not None
    n_chunks = tg // CHUNK
    chunks_per_step = 8 if n_chunks % 8 == 0 else 1
    qspec = pl.BlockSpec((None, tg, LANES), lambda b, p, g: (b, g, p))
    vspec = pl.BlockSpec((None, tg, 2 * GLA_DV), lambda b, p, g: (b, g, p))
    sspec = pl.BlockSpec((None, 2, GLA_DK, GLA_DV), lambda b, p, g: (b, p, 0, 0))
    s0spec = pl.BlockSpec((None, None, 2, GLA_DK, GLA_DV), lambda b, p, g: (layer, b, p, 0, 0))
    in_specs = [qspec, qspec, qspec, vspec] + ([s0spec] if has_init else [])
    args = (gq, gk, la, gv) + ((s0,) if has_init else ())
    return pl.pallas_call(
        functools.partial(_gla_kernel, n_chunks=n_chunks, chunks_per_step=chunks_per_step, has_init=has_init),
        grid=(bsz, GLA_HEADS // 2, t // tg),
        in_specs=in_specs,
        out_specs=[vspec, sspec],
        out_shape=[jax.ShapeDtypeStruct((bsz, t, GLA_WIDTH), F32),
                   jax.ShapeDtypeStruct((bsz, GLA_HEADS, GLA_DK, GLA_DV), F32)],
        scratch_shapes=[pltpu.VMEM((2 * GLA_DK, GLA_DV), F32), pltpu.VMEM((chunks_per_step, CHUNK, LANES), F32)],
        compiler_params=pltpu.CompilerParams(
            dimension_semantics=("parallel", "parallel", "arbitrary"), vmem_limit_bytes=VMEM_LIMIT),
        name="gla",
    )(*args)


def _cumsum_matrix(tk):
    r = _iota((tk, 2 * tk), 0)
    c = _iota((tk, 2 * tk), 1)
    return jnp.where((c >= tk) | (r > c), 1.0, 0.0).astype(BF16)


def _sb_unit(streams):
    zs = [[_dot_nt(q2, kj) for kj, _, _, _ in blocks] for q2, blocks, _, _ in streams]
    sps = []
    for z_list, (_, blocks, _, _) in zip(zs, streams):
        sp_list = []
        for z, (_, _, _, mask) in zip(z_list, blocks):
            sp = _softplus(z)
            sp_list.append(sp if mask is None else jnp.where(mask, sp, 0.0))
        sps.append(sp_list)
    crs = []
    for sp_list, (q2, blocks, _, _) in zip(sps, streams):
        cr_list = [None] * len(blocks)
        rows = q2.shape[0]
        for tk in sorted({b[0].shape[0] for b in blocks}):
            idx = [n for n, b in enumerate(blocks) if b[0].shape[0] == tk]
            u = blocks[idx[0]][2]
            cr = _dot(jnp.concatenate([sp_list[n] for n in idx], axis=0).astype(BF16), u)
            for m, n in enumerate(idx):
                cr_list[n] = cr[m * rows:(m + 1) * rows]
        crs.append(cr_list)
    ws = []
    carries = []
    for z_list, sp_list, cr_list, (_, blocks, carry, _) in zip(zs, sps, crs, streams):
        w_list = []
        for z, sp, cr, (kj, _, _, mask) in zip(z_list, sp_list, cr_list, blocks):
            tk = kj.shape[0]
            w = jnp.exp(z - sp - cr[:, :tk] - carry[:, :tk])
            if mask is not None:
                w = jnp.where(mask, w, 0.0)
            w_list.append(w.astype(BF16))
            rs = cr[:, tk:]
            if tk < LANES:
                rs = jnp.concatenate([rs] * (LANES // tk), axis=1)
            carry = carry + rs
        ws.append(w_list)
        carries.append(carry)
    out = []
    for w_list, carry, (_, blocks, _, acc) in zip(ws, carries, streams):
        pv = None
        for w, (_, vj, _, _) in zip(w_list, blocks):
            d = _dot(w, vj)
            pv = d if pv is None else pv + d
        out.append((carry, acc + pv))
    return out


def _sb_kernel(*refs, tq, n_sub, n_past_blocks):
    if n_past_blocks:
        q_ref, k_ref, v_ref, pk_ref, pv_ref, o_ref = refs
    else:
        q_ref, k_ref, v_ref, o_ref = refs
    i = pl.program_id(2)
    lane = _iota((tq, LANES), 1)
    u_own = _cumsum_matrix(tq)
    strict = _iota((2 * tq, tq), 1) < _iota((2 * tq, tq), 0) % tq
    zeros = jnp.zeros((2 * tq, LANES), F32)

    def stacked_heads(r):
        q = q_ref[r * tq:(r + 1) * tq, :]
        zero_q = jnp.zeros_like(q)
        return jnp.concatenate([jnp.where(lane < SB_HD, q, zero_q), jnp.where(lane < SB_HD, zero_q, q)], axis=0)

    q2s = [stacked_heads(r) for r in range(n_sub)]

    if n_past_blocks:
        u_past = _cumsum_matrix(LANES)
        pair = pl.program_id(1)

        def past_block(ref, j):
            halves = [ref[pl.ds(j * LANES * SB_HEADS + 2 * pair + hh, LANES, stride=SB_HEADS), :] for hh in range(2)]
            return jnp.concatenate(halves, axis=1).astype(BF16)

        blocks = [(k_ref[...], v_ref[...], u_own, strict)]
        for j in reversed(range(n_past_blocks)):
            blocks.append((past_block(pk_ref, j), past_block(pv_ref, j), u_past, None))
        accs = [_sb_unit([(q2s[0], blocks, zeros, zeros)])[0][1]]
    else:
        def load_unit(r, n, first):
            blocks = []
            for kk in range(SB_UNIT):
                j = i * n_sub + r - n * SB_UNIT - kk
                rows = pl.ds(pl.multiple_of(jnp.maximum(j, 0) * tq, tq), tq)
                vj = v_ref[rows, :]
                diag = first and kk == 0
                if not diag:
                    vj = jnp.where(j >= 0, vj, jnp.zeros_like(vj))
                blocks.append((k_ref[rows, :], vj, u_own, strict if diag else None))
            return blocks

        def min_carry(state):
            m = state[0][0]
            for carry, _ in state[1:]:
                m = jnp.minimum(m, carry)
            return jnp.min(m)

        state = _sb_unit([(q2s[r], load_unit(r, 0, True), zeros, zeros) for r in range(n_sub)])
        n_units = (i * n_sub + n_sub - 1 + SB_UNIT) // SB_UNIT

        def cond(st):
            return (st[0] < n_units) & (st[1] < SB_ZERO_LOG)

        def body(st):
            n, _, state = st
            state = _sb_unit([(q2s[r], load_unit(r, n, False), *state[r]) for r in range(n_sub)])
            return n + 1, min_carry(state), state

        state = lax.while_loop(cond, body, (jnp.int32(1), min_carry(state), state))[2]
        accs = [acc for _, acc in state]
    for r, acc in enumerate(accs):
        o_ref[r * tq:(r + 1) * tq, :] = jnp.where(lane < SB_HD, acc[:tq], acc[tq:]).astype(o_ref.dtype)


def _sb(q, k, v, past_k, past_v, layer, tq, n_sub):
    bsz, t, _ = q.shape
    n_past_blocks = 0 if past_k is None else past_k.shape[2] // (LANES * SB_HEADS)
    qspec = pl.BlockSpec((None, tq * n_sub, LANES), lambda b, p, i: (b, i, p))
    kspec = pl.BlockSpec((None, t, LANES), lambda b, p, i: (b, 0, p))
    in_specs = [qspec, kspec, kspec]
    args = (q, k, v)
    if n_past_blocks:
        pspec = pl.BlockSpec((None, None) + past_k.shape[2:], lambda b, p, i: (layer, b, 0, 0))
        in_specs += [pspec, pspec]
        args += (past_k, past_v)
    return pl.pallas_call(
        functools.partial(_sb_kernel, tq=tq, n_sub=n_sub, n_past_blocks=n_past_blocks),
        grid=(bsz, SB_WIDTH // LANES, t // (tq * n_sub)),
        in_specs=in_specs,
        out_specs=qspec,
        out_shape=jax.ShapeDtypeStruct((bsz, t, SB_WIDTH), BF16),
        compiler_params=pltpu.CompilerParams(
            dimension_semantics=("parallel", "parallel", "arbitrary"), vmem_limit_bytes=VMEM_LIMIT),
        name="sb",
    )(*args)


def _finish_kernel(x_ref, og_ref, gate_ref, osb_ref, gg_ref, wo_ref, g2_ref, wup_ref, wdn_ref, y_ref, *, ff_block):
    og = og_ref[...]
    normed = [_rmsnorm_rows(og[:, h * GLA_DV:(h + 1) * GLA_DV], gg_ref[...]) for h in range(GLA_HEADS)]
    gate = gate_ref[...]
    mix_g = (jnp.concatenate(normed, axis=-1) * (gate * jax.nn.sigmoid(gate))).astype(BF16)
    h = x_ref[...] + _dot(mix_g, wo_ref[:GLA_WIDTH, :]) + _dot(osb_ref[...], wo_ref[GLA_WIDTH:, :])
    hn = _rmsnorm_rows(h, g2_ref[...]).astype(BF16)
    mlp = None
    for j in range(D_FF // ff_block):
        cols = slice(j * ff_block, (j + 1) * ff_block)
        u = _dot(hn, wup_ref[:, cols])
        a = jnp.square(jnp.maximum(u, 0.0)).astype(BF16)
        d = _dot(a, wdn_ref[cols, :])
        mlp = d if mlp is None else mlp + d
    y_ref[...] = h + mlp


def _finish(x2d, og, gate, osb, pw, layer, tm):
    n = x2d.shape[0]
    row = lambda w: pl.BlockSpec((tm, w), lambda i: (i, 0))
    per_layer = lambda a: pl.BlockSpec((None,) + a.shape[1:], lambda i: (layer,) + (0,) * (a.ndim - 1))
    weights = (pw['gg'], pw['wo'], pw['g2'], pw['wup'], pw['wdn'])
    return pl.pallas_call(
        functools.partial(_finish_kernel, ff_block=1024),
        grid=(n // tm,),
        in_specs=[row(D_MODEL), row(GLA_WIDTH), row(GLA_WIDTH), row(SB_WIDTH)] + [per_layer(w) for w in weights],
        out_specs=row(D_MODEL),
        out_shape=jax.ShapeDtypeStruct((n, D_MODEL), F32),
        compiler_params=pltpu.CompilerParams(dimension_semantics=("parallel",), vmem_limit_bytes=VMEM_LIMIT),
        name="finish",
    )(x2d, og, gate, osb, *weights)


def _prepare_weights(norm1_g, w_in, w_a2, b_a2, q_norm_g, k_norm_g, gla_norm_g, w_out, norm2_g, w_up, w_down):
    gate_end = _GATE_COLS + GATE_RANK
    w_in = w_in.astype(BF16)
    win = jnp.concatenate(
        [w_in[:, :, :gate_end], jnp.zeros((DEPTH, D_MODEL, LANES - GATE_RANK), BF16), w_in[:, :, gate_end:]], axis=2)
    row = lambda a: a.reshape(DEPTH, 1, a.shape[-1])
    return {
        'g1': row(norm1_g),
        'win': win,
        'wa2': jnp.pad(w_a2, ((0, 0), (0, LANES - GATE_RANK), (0, 0))).astype(BF16),
        'ba2': row(b_a2),
        'qg': row(jnp.tile(q_norm_g, (1, SB_HEADS))),
        'kg': row(jnp.tile(k_norm_g, (1, SB_HEADS))),
        'gg': row(gla_norm_g),
        'wo': w_out.astype(BF16),
        'g2': row(norm2_g),
        'wup': w_up.astype(BF16),
        'wdn': w_down.astype(BF16),
    }


def _stream_layer(x, pw, layer, kv_bufs, past_k, past_v, s0, tm, tf, tg, tq, n_sub):
    bsz, t, _ = x.shape
    x2d = x.reshape(bsz * t, D_MODEL)
    gq, gk, la, gv, gate, sq, skb, svb, skf, svf = _project(x2d, pw, layer, kv_bufs, tm)
    r3 = lambda a: a.reshape(bsz, t, a.shape[-1])
    o_gla, s_new = _gla(r3(gq), r3(gk), r3(la), r3(gv), s0, layer, tg)
    o_sb = _sb(r3(sq), r3(skb), r3(svb), past_k, past_v, layer, tq, n_sub)
    y = _finish(x2d, o_gla.reshape(bsz * t, GLA_WIDTH), gate, o_sb.reshape(bsz * t, SB_WIDTH), pw, layer, tf)
    return y.reshape(bsz, t, D_MODEL), (skf, svf), s_new


def kernel(x_prompt, x_sample, cache_sb_k, cache_sb_v, state_gla, norm1_g, w_in, w_a2, b_a2, q_norm_g, k_norm_g,
           gla_norm_g, w_out, norm2_g, w_up, w_down):
    pw = _prepare_weights(norm1_g, w_in, w_a2, b_a2, q_norm_g, k_norm_g, gla_norm_g, w_out, norm2_g, w_up, w_down)
    nb, past = cache_sb_k.shape[1], cache_sb_k.shape[2]
    past_k = cache_sb_k.reshape(DEPTH, nb, past * SB_HEADS, SB_HD)
    past_v = cache_sb_v.reshape(DEPTH, nb, past * SB_HEADS, SB_HD)
    xp, xs = x_prompt, x_sample
    kv_p, kv_s, states_p, states_s = None, None, [], []
    for layer in range(DEPTH):
        xp, kv_p, sp = _stream_layer(xp, pw, layer, kv_p, None, None, None, tm=512, tf=256, tg=512, tq=128, n_sub=4)
        xs, kv_s, ss = _stream_layer(xs, pw, layer, kv_s, past_k, past_v, state_gla, tm=512, tf=256, tg=CHUNK,
                                     tq=CHUNK, n_sub=1)
        states_p.append(sp)
        states_s.append(ss)
    cache = lambda a, x: a.reshape((DEPTH,) + x.shape[:2] + (SB_HEADS, SB_HD))
    return (xp, xs, cache(kv_p[0], xp), cache(kv_p[1], xp), jnp.stack(states_p),
            cache(kv_s[0], xs), cache(kv_s[1], xs), jnp.stack(states_s))
```

```python
import functools

import jax
import jax.numpy as jnp
from jax import lax
from jax.experimental import pallas as pl
from jax.experimental.pallas import tpu as pltpu

D_MODEL = 1024
DEPTH = 2
CHUNK = 64
GLA_HEADS = 4
GLA_DK = 64
GLA_DV = 128
GLA_KEY = GLA_HEADS * GLA_DK
GLA_WIDTH = GLA_HEADS * GLA_DV
GATE_RANK = 16
GATE_NORM = 16.0
SB_HEADS = 8
SB_HD = 64
SB_WIDTH = SB_HEADS * SB_HD
D_FF = 4 * D_MODEL
EPS = 1e-6

LANES = 128
ROW_TILE = 16
SB_UNIT = 3
SB_ZERO_LOG = 110.0
LOG2E = 1.4426950408889634
VMEM_LIMIT = 56 * 1024 * 1024

_GATE_COLS = 2 * GLA_KEY + 2 * GLA_WIDTH
_COL_QK = (0, 2 * GLA_KEY)
_COL_GV = (2 * GLA_KEY, 2 * GLA_KEY + GLA_WIDTH)
_COL_GATE = (2 * GLA_KEY + GLA_WIDTH, _GATE_COLS)
_COL_ALR = (_GATE_COLS, _GATE_COLS + LANES)
GLA_COLS = _COL_ALR[1]
SB_COL0 = _GATE_COLS + GATE_RANK
_COL_SQ = (0, SB_WIDTH)
_COL_SK = (SB_WIDTH, 2 * SB_WIDTH)
_COL_SV = (2 * SB_WIDTH, 3 * SB_WIDTH)

F32 = jnp.float32
BF16 = jnp.bfloat16


def _dot(a, b):
    return jnp.dot(a, b, preferred_element_type=F32)


def _dot_nt(a, b):
    return lax.dot_general(a, b, (((1,), (1,)), ((), ())), preferred_element_type=F32)


def _iota(shape, dim):
    return lax.broadcasted_iota(jnp.int32, shape, dim)


def _split3(x):
    p1 = x.astype(BF16)
    r1 = x - p1.astype(F32)
    p2 = r1.astype(BF16)
    p3 = (r1 - p2.astype(F32)).astype(BF16)
    return p1, p2, p3


def _split2(x):
    p1 = x.astype(BF16)
    p2 = (x - p1.astype(F32)).astype(BF16)
    return p1, p2


def _softplus(z):
    return jnp.maximum(z, 0.0) + jnp.log(1.0 + jnp.exp(-jnp.abs(z)))


def _rmsnorm_rows(x, g):
    ms = jnp.mean(x * x, axis=-1, keepdims=True)
    return x * lax.rsqrt(ms + EPS) * g


def _project_kernel(*refs, aliased):
    x_ref, g1_ref, wgla_ref, wsb_ref, wa2_ref, ba2_ref, qg_ref, kg_ref = refs[:8]
    gq_ref, gk_ref, la_ref, gv_ref, gate_ref, sq_ref, skb_ref, svb_ref, k4_ref, v4_ref = refs[8 + 2 * aliased:]
    xn = _rmsnorm_rows(x_ref[...], g1_ref[...]).astype(BF16)
    proj = lambda w_ref, cols: _dot(xn, w_ref[:, cols[0]:cols[1]])

    qk = proj(wgla_ref, _COL_QK)
    gv = proj(wgla_ref, _COL_GV)
    gate = proj(wgla_ref, _COL_GATE)
    alr = proj(wgla_ref, _COL_ALR)
    sq = proj(wsb_ref, _COL_SQ)
    sk = proj(wsb_ref, _COL_SK)
    sv = proj(wsb_ref, _COL_SV)
    y = _dot(alr.astype(BF16), wa2_ref[...]) + ba2_ref[...]

    r = _iota((LANES, LANES), 0) // SB_HD
    c = _iota((LANES, LANES), 1) // SB_HD
    ones_bd = jnp.where(r == c, 1.0, 0.0).astype(BF16)

    def head_mean_squares(s):
        out = []
        for j in range(SB_WIDTH // LANES):
            sj = s[:, j * LANES:(j + 1) * LANES]
            hi, lo = _split2(sj * sj)
            out.append((_dot(hi, ones_bd) + _dot(lo, ones_bd)) * (1.0 / SB_HD))
        return jnp.concatenate(out, axis=-1)

    def store_cache_layout(ref, a):
        rows = ref.reshape(a.shape[0] * SB_HEADS, SB_HD)
        for h in range(SB_HEADS):
            rows[pl.ds(h, a.shape[0], stride=SB_HEADS), :] = a[:, h * SB_HD:(h + 1) * SB_HD]

    ms_q = head_mean_squares(sq)
    ms_k = head_mean_squares(sk)

    gq_ref[...] = qk[:, :GLA_KEY] * (GLA_DK ** -0.5)
    gk_ref[...] = qk[:, GLA_KEY:]
    gv_ref[...] = gv.astype(BF16)
    gate_ref[...] = gate
    svb_ref[...] = sv.astype(BF16)
    store_cache_layout(v4_ref, sv)
    la_ref[...] = (jnp.minimum(y, 0.0) - jnp.log(1.0 + jnp.exp(-jnp.abs(y)))) * (1.0 / GATE_NORM)
    sq_ref[...] = (sq * lax.rsqrt(ms_q + EPS) * qg_ref[...] * (SB_HD ** -0.5)).astype(BF16)
    sk_n = sk * lax.rsqrt(ms_k + EPS) * kg_ref[...]
    skb_ref[...] = sk_n.astype(BF16)
    store_cache_layout(k4_ref, sk_n)


def _project(x2d, bsz, pw, layer, kv_bufs, tm):
    n = x2d.shape[0]
    t = n // bsz
    aliased = kv_bufs is not None
    row = lambda w: pl.BlockSpec((tm, w), lambda i: (i, 0))
    per_layer = lambda a: pl.BlockSpec((None,) + a.shape[1:], lambda i: (layer,) + (0,) * (a.ndim - 1))
    weights = (pw['g1'], pw['win'], pw['wsb'], pw['wa2'], pw['ba2'], pw['qg'], pw['kg'])
    weight_specs = [per_layer(w) for w in weights]
    weight_specs[1] = pl.BlockSpec((None, D_MODEL, GLA_COLS), lambda i: (layer, 0, 0))
    out_cols = ((GLA_KEY, F32), (GLA_KEY, F32), (GLA_KEY, F32), (GLA_WIDTH, BF16), (GLA_WIDTH, F32),
                (SB_WIDTH, BF16), (SB_WIDTH, BF16), (SB_WIDTH, BF16))
    if tm <= t:
        kv_spec = pl.BlockSpec((None, 1, tm, SB_HEADS, SB_HD), lambda i: (layer, i // (t // tm), i % (t // tm), 0, 0))
    else:
        kv_spec = pl.BlockSpec((None, tm // t, t, SB_HEADS, SB_HD), lambda i: (layer, i, 0, 0, 0))
    kv_shape = jax.ShapeDtypeStruct((DEPTH, bsz, t, SB_HEADS, SB_HD), F32)
    n_in = 1 + len(weights)
    return pl.pallas_call(
        functools.partial(_project_kernel, aliased=aliased),
        grid=(n // tm,),
        in_specs=[row(D_MODEL)] + weight_specs + ([pl.BlockSpec(memory_space=pl.ANY)] * 2 if aliased else []),
        out_specs=[row(w) for w, _ in out_cols] + [kv_spec, kv_spec],
        out_shape=[jax.ShapeDtypeStruct((n, w), dt) for w, dt in out_cols] + [kv_shape, kv_shape],
        input_output_aliases={n_in: len(out_cols), n_in + 1: len(out_cols) + 1} if aliased else {},
        compiler_params=pltpu.CompilerParams(dimension_semantics=("parallel",), vmem_limit_bytes=VMEM_LIMIT),
        name="project",
    )(x2d, *weights, *(kv_bufs if aliased else ()))


def _gla_kernel(*refs, n_chunks, chunks_per_step, has_init):
    if has_init:
        q_ref, k_ref, la_ref, v_ref, s0_ref, o_ref, sout_ref, s_scr, b_scr = refs
    else:
        q_ref, k_ref, la_ref, v_ref, o_ref, sout_ref, s_scr, b_scr = refs
    g = pl.program_id(2)

    @pl.when(g == 0)
    def _():
        if has_init:
            s_scr[...] = s0_ref[...].reshape(2 * GLA_DK, GLA_DV)
        else:
            s_scr[...] = jnp.zeros_like(s_scr)

    rr = _iota((CHUNK, CHUNK), 0)
    cc = _iota((CHUNK, CHUNK), 1)
    tri = jnp.where(cc <= rr, 1.0, 0.0).astype(BF16)
    r = _iota((2 * LANES, 2 * LANES), 0) // GLA_DK
    c = _iota((2 * LANES, 2 * LANES), 1) // GLA_DK
    ones_bd2 = jnp.where(r == c, 1.0, 0.0).astype(BF16)
    lane = _iota((CHUNK, LANES), 1)
    lane_t = lane % CHUNK
    tile_t = _iota((ROW_TILE, LANES), 1) % CHUNK
    srow = _iota((CHUNK, LANES), 0)
    head_a = lane < GLA_DK

    n_sub = CHUNK // ROW_TILE
    sub = lambda g: slice(g * ROW_TILE, (g + 1) * ROW_TILE)

    def operands(slot, base, b):
        b2 = b * LOG2E
        b_scr[slot] = b2
        rows = pl.ds(base, CHUNK)
        q = q_ref[rows, :]
        k = k_ref[rows, :]

        b_end = jnp.concatenate(
            [jnp.broadcast_to(b[(g + 1) * ROW_TILE - 1:(g + 1) * ROW_TILE], (ROW_TILE, LANES)) for g in range(n_sub)],
            axis=0)
        k_dec = (k * jnp.exp(b_end - b)).astype(BF16)
        q_dec_rhs = []
        for g in range(n_sub - 1):
            lo = (g + 1) * ROW_TILE
            q_dec = q[lo:] * jnp.exp(b[lo:] - b[lo - 1:lo])
            pad = jnp.zeros((lo, LANES), F32)
            in_a = _iota((CHUNK - lo, LANES), 1) < GLA_DK
            q_dec_rhs.append(
                jnp.concatenate([pad, jnp.where(in_a, q_dec, 0.0), pad, jnp.where(in_a, 0.0, q_dec)],
                                axis=0).astype(BF16))

        es = []
        for t in range(CHUNK):
            g = t // ROW_TILE
            brow = b_scr[slot, t:t + 1, :]
            qrow = q_ref[pl.ds(base + t, 1), :]
            es.append(jnp.exp2(brow - b2[sub(g)]).astype(BF16) * (qrow * k[sub(g)]).astype(BF16))
        diag_lhs = jnp.concatenate(
            [jnp.concatenate([es[2 * p], es[2 * p + 1]], axis=1) for p in range(CHUNK // 2)], axis=0)

        qe = q * jnp.exp(b)
        qe_a = jnp.where(head_a, qe, 0.0).astype(BF16)
        qe_b = jnp.where(head_a, 0.0, qe).astype(BF16)
        b_t = b.T
        b_last = b_t[:, CHUNK - 1:CHUNK]
        kd_t = (k.T * jnp.exp(b_last - b_t)).astype(BF16)
        return k_dec, q_dec_rhs, diag_lhs, qe_a, qe_b, kd_t, jnp.exp(b_last)

    def scores(off_tiles, red):
        pt_tiles = list(off_tiles) + [jnp.zeros((ROW_TILE, LANES), F32)]
        for t in range(CHUNK):
            g, p, half = t // ROW_TILE, t // 2, t % 2
            r_t = red[p * ROW_TILE:(p + 1) * ROW_TILE, half * LANES:(half + 1) * LANES]
            pt_tiles[g] = jnp.where(tile_t == t, r_t, pt_tiles[g])
        pt = jnp.concatenate(pt_tiles, axis=0)
        return jnp.where(srow <= lane_t, pt, 0.0)

    def step_body(si, carry):
        slots = range(chunks_per_step)
        bases = [pl.multiple_of((si * chunks_per_step + slot) * CHUNK, CHUNK) for slot in slots]
        bs = []
        for base in bases:
            p1, p2, p3 = _split3(la_ref[pl.ds(base, CHUNK), :])
            bs.append(_dot(tri, p1) + _dot(tri, p2) + _dot(tri, p3))
        ops = [operands(slot, bases[slot], bs[slot]) for slot in slots]
        v2s = [v_ref[pl.ds(base, CHUNK), :] for base in bases]
        offs = [[_dot_nt(op[0][sub(g)], op[1][g]) for g in range(n_sub - 1)] for op in ops]
        reds = [_dot(op[2], ones_bd2) for op in ops]
        upds = [_dot(op[5], v2) for op, v2 in zip(ops, v2s)]
        pts = [scores(off, red).T.astype(BF16) for off, red in zip(offs, reds)]
        o_intras = [_dot(pt, v2) for pt, v2 in zip(pts, v2s)]

        s = s_scr[...]
        for slot in slots:
            _, _, _, qe_a, qe_b, _, decay = ops[slot]
            rows = pl.ds(bases[slot], CHUNK)
            s_bf = s.astype(BF16)
            o_ref[rows, :GLA_DV] = o_intras[slot][:CHUNK, :GLA_DV] + _dot(qe_a, s_bf)
            o_ref[rows, GLA_DV:] = o_intras[slot][CHUNK:, GLA_DV:] + _dot(qe_b, s_bf)
            upd = upds[slot]
            s = decay * s + jnp.concatenate([upd[:GLA_DK, :GLA_DV], upd[GLA_DK:, GLA_DV:]], axis=0)
        s_scr[...] = s
        return carry

    lax.fori_loop(0, n_chunks // chunks_per_step, step_body, 0)

    @pl.when(g == pl.num_programs(2) - 1)
    def _():
        sout_ref[...] = s_scr[...].reshape(2, GLA_DK, GLA_DV)


def _gla(gq, gk, la, gv, s0, layer, tg):
    bsz, t, _ = gq.shape
    has_init = s0 is not None
    n_chunks = tg // CHUNK
    chunks_per_step = 8 if n_chunks % 8 == 0 else 1
    qspec = pl.BlockSpec((None, tg, LANES), lambda b, p, g: (b, g, p))
    vspec = pl.BlockSpec((None, tg, 2 * GLA_DV), lambda b, p, g: (b, g, p))
    sspec = pl.BlockSpec((None, 2, GLA_DK, GLA_DV), lambda b, p, g: (b, p, 0, 0))
    s0spec = pl.BlockSpec((None, None, 2, GLA_DK, GLA_DV), lambda b, p, g: (layer, b, p, 0, 0))
    in_specs = [qspec, qspec, qspec, vspec] + ([s0spec] if has_init else [])
    args = (gq, gk, la, gv) + ((s0,) if has_init else ())
    return pl.pallas_call(
        functools.partial(_gla_kernel, n_chunks=n_chunks, chunks_per_step=chunks_per_step, has_init=has_init),
        grid=(bsz, GLA_HEADS // 2, t // tg),
        in_specs=in_specs,
        out_specs=[vspec, sspec],
        out_shape=[jax.ShapeDtypeStruct((bsz, t, GLA_WIDTH), F32),
                   jax.ShapeDtypeStruct((bsz, GLA_HEADS, GLA_DK, GLA_DV), F32)],
        scratch_shapes=[pltpu.VMEM((2 * GLA_DK, GLA_DV), F32), pltpu.VMEM((chunks_per_step, CHUNK, LANES), F32)],
        compiler_params=pltpu.CompilerParams(
            dimension_semantics=("parallel", "parallel", "arbitrary"), vmem_limit_bytes=VMEM_LIMIT),
        name="gla",
    )(*args)


def _cumsum_matrix(tk):
    r = _iota((tk, 2 * tk), 0)
    c = _iota((tk, 2 * tk), 1)
    return jnp.where((c >= tk) | (r > c), 1.0, 0.0).astype(BF16)


def _sb_unit(streams):
    zs = [[_dot_nt(q2, kj) for kj, _, _, _ in blocks] for q2, blocks, _, _ in streams]
    sps = []
    for z_list, (_, blocks, _, _) in zip(zs, streams):
        sp_list = []
        for z, (_, _, _, mask) in zip(z_list, blocks):
            sp = _softplus(z)
            sp_list.append(sp if mask is None else jnp.where(mask, sp, 0.0))
        sps.append(sp_list)
    crs = []
    for sp_list, (q2, blocks, _, _) in zip(sps, streams):
        cr_list = [None] * len(blocks)
        rows = q2.shape[0]
        for tk in sorted({b[0].shape[0] for b in blocks}):
            idx = [n for n, b in enumerate(blocks) if b[0].shape[0] == tk]
            u = blocks[idx[0]][2]
            cr = _dot(jnp.concatenate([sp_list[n] for n in idx], axis=0).astype(BF16), u)
            for m, n in enumerate(idx):
                cr_list[n] = cr[m * rows:(m + 1) * rows]
        crs.append(cr_list)
    ws = []
    carries = []
    for z_list, sp_list, cr_list, (_, blocks, carry, _) in zip(zs, sps, crs, streams):
        w_list = []
        for z, sp, cr, (kj, _, _, mask) in zip(z_list, sp_list, cr_list, blocks):
            tk = kj.shape[0]
            w = jnp.exp(z - sp - cr[:, :tk] - carry[:, :tk])
            if mask is not None:
                w = jnp.where(mask, w, 0.0)
            w_list.append(w.astype(BF16))
            rs = cr[:, tk:]
            if tk < LANES:
                rs = jnp.concatenate([rs] * (LANES // tk), axis=1)
            carry = carry + rs
        ws.append(w_list)
        carries.append(carry)
    out = []
    for w_list, carry, (_, blocks, _, acc) in zip(ws, carries, streams):
        pv = None
        for w, (_, vj, _, _) in zip(w_list, blocks):
            d = _dot(w, vj)
            pv = d if pv is None else pv + d
        out.append((carry, acc + pv))
    return out


def _sb_kernel(*refs, tq, n_sub, n_past_blocks):
    if n_past_blocks:
        q_ref, k_ref, v_ref, pk_ref, pv_ref, o_ref = refs
    else:
        q_ref, k_ref, v_ref, o_ref = refs
    i = pl.program_id(2)
    lane = _iota((tq, LANES), 1)
    u_own = _cumsum_matrix(tq)
    strict = _iota((2 * tq, tq), 1) < _iota((2 * tq, tq), 0) % tq
    zeros = jnp.zeros((2 * tq, LANES), F32)

    def stacked_heads(r):
        q = q_ref[r * tq:(r + 1) * tq, :]
        zero_q = jnp.zeros_like(q)
        return jnp.concatenate([jnp.where(lane < SB_HD, q, zero_q), jnp.where(lane < SB_HD, zero_q, q)], axis=0)

    q2s = [stacked_heads(r) for r in range(n_sub)]

    if n_past_blocks:
        u_past = _cumsum_matrix(LANES)
        pair = pl.program_id(1)

        def past_block(ref, j):
            halves = [ref[pl.ds(j * LANES * SB_HEADS + 2 * pair + hh, LANES, stride=SB_HEADS), :] for hh in range(2)]
            return jnp.concatenate(halves, axis=1).astype(BF16)

        blocks = [(k_ref[...], v_ref[...], u_own, strict)]
        for j in reversed(range(n_past_blocks)):
            blocks.append((past_block(pk_ref, j), past_block(pv_ref, j), u_past, None))
        accs = [_sb_unit([(q2s[0], blocks, zeros, zeros)])[0][1]]
    else:
        def load_unit(r, n, first):
            blocks = []
            for kk in range(SB_UNIT):
                j = i * n_sub + r - n * SB_UNIT - kk
                rows = pl.ds(pl.multiple_of(jnp.maximum(j, 0) * tq, tq), tq)
                vj = v_ref[rows, :]
                diag = first and kk == 0
                if not diag:
                    vj = jnp.where(j >= 0, vj, jnp.zeros_like(vj))
                blocks.append((k_ref[rows, :], vj, u_own, strict if diag else None))
            return blocks

        def min_carry(state):
            m = state[0][0]
            for carry, _ in state[1:]:
                m = jnp.minimum(m, carry)
            return jnp.min(m)

        state = _sb_unit([(q2s[r], load_unit(r, 0, True), zeros, zeros) for r in range(n_sub)])
        n_units = (i * n_sub + n_sub - 1 + SB_UNIT) // SB_UNIT

        def cond(st):
            return (st[0] < n_units) & (st[1] < SB_ZERO_LOG)

        def body(st):
            n, _, state = st
            state = _sb_unit([(q2s[r], load_unit(r, n, False), *state[r]) for r in range(n_sub)])
            return n + 1, min_carry(state), state

        state = lax.while_loop(cond, body, (jnp.int32(1), min_carry(state), state))[2]
        accs = [acc for _, acc in state]
    for r, acc in enumerate(accs):
        o_ref[r * tq:(r + 1) * tq, :] = jnp.where(lane < SB_HD, acc[:tq], acc[tq:]).astype(o_ref.dtype)


def _sb(q, k, v, past_k, past_v, layer, tq, n_sub):
    bsz, t, _ = q.shape
    n_past_blocks = 0 if past_k is None else past_k.shape[2] // (LANES * SB_HEADS)
    qspec = pl.BlockSpec((None, tq * n_sub, LANES), lambda b, p, i: (b, i, p))
    kspec = pl.BlockSpec((None, t, LANES), lambda b, p, i: (b, 0, p))
    in_specs = [qspec, kspec, kspec]
    args = (q, k, v)
    if n_past_blocks:
        pspec = pl.BlockSpec((None, None) + past_k.shape[2:], lambda b, p, i: (layer, b, 0, 0))
        in_specs += [pspec, pspec]
        args += (past_k, past_v)
    return pl.pallas_call(
        functools.partial(_sb_kernel, tq=tq, n_sub=n_sub, n_past_blocks=n_past_blocks),
        grid=(bsz, SB_WIDTH // LANES, t // (tq * n_sub)),
        in_specs=in_specs,
        out_specs=qspec,
        out_shape=jax.ShapeDtypeStruct((bsz, t, SB_WIDTH), BF16),
        compiler_params=pltpu.CompilerParams(
            dimension_semantics=("parallel", "parallel", "arbitrary"), vmem_limit_bytes=VMEM_LIMIT),
        name="sb",
    )(*args)


def _finish_kernel(x_ref, og_ref, gate_ref, osb_ref, gg_ref, wo_ref, g2_ref, wup_ref, wdn_ref, y_ref, *, ff_block):
    og = og_ref[...]
    normed = [_rmsnorm_rows(og[:, h * GLA_DV:(h + 1) * GLA_DV], gg_ref[...]) for h in range(GLA_HEADS)]
    gate = gate_ref[...]
    mix_g = (jnp.concatenate(normed, axis=-1) * (gate * jax.nn.sigmoid(gate))).astype(BF16)
    h = x_ref[...] + _dot(mix_g, wo_ref[:GLA_WIDTH, :]) + _dot(osb_ref[...], wo_ref[GLA_WIDTH:, :])
    hn = _rmsnorm_rows(h, g2_ref[...]).astype(BF16)
    mlp = None
    for j in range(D_FF // ff_block):
        cols = slice(j * ff_block, (j + 1) * ff_block)
        u = _dot(hn, wup_ref[:, cols])
        a = jnp.square(jnp.maximum(u, 0.0)).astype(BF16)
        d = _dot(a, wdn_ref[cols, :])
        mlp = d if mlp is None else mlp + d
    y_ref[...] = h + mlp


def _finish(x2d, og, gate, osb, pw, layer, tm):
    n = x2d.shape[0]
    row = lambda w: pl.BlockSpec((tm, w), lambda i: (i, 0))
    per_layer = lambda a: pl.BlockSpec((None,) + a.shape[1:], lambda i: (layer,) + (0,) * (a.ndim - 1))
    weights = (pw['gg'], pw['wo'], pw['g2'], pw['wup'], pw['wdn'])
    return pl.pallas_call(
        functools.partial(_finish_kernel, ff_block=1024),
        grid=(n // tm,),
        in_specs=[row(D_MODEL), row(GLA_WIDTH), row(GLA_WIDTH), row(SB_WIDTH)] + [per_layer(w) for w in weights],
        out_specs=row(D_MODEL),
        out_shape=jax.ShapeDtypeStruct((n, D_MODEL), F32),
        compiler_params=pltpu.CompilerParams(dimension_semantics=("parallel",), vmem_limit_bytes=VMEM_LIMIT),
        name="finish",
    )(x2d, og, gate, osb, *weights)


def _prepare_weights(norm1_g, w_in, w_a2, b_a2, q_norm_g, k_norm_g, gla_norm_g, w_out, norm2_g, w_up, w_down):
    w_in = w_in.astype(BF16)
    row = lambda a: a.reshape(DEPTH, 1, a.shape[-1])
    return {
        'g1': row(norm1_g),
        'win': w_in,
        'wsb': w_in[:, :, SB_COL0:],
        'wa2': jnp.pad(w_a2, ((0, 0), (0, LANES - GATE_RANK), (0, 0))).astype(BF16),
        'ba2': row(b_a2),
        'qg': row(jnp.tile(q_norm_g, (1, SB_HEADS))),
        'kg': row(jnp.tile(k_norm_g, (1, SB_HEADS))),
        'gg': row(gla_norm_g),
        'wo': w_out.astype(BF16),
        'g2': row(norm2_g),
        'wup': w_up.astype(BF16),
        'wdn': w_down.astype(BF16),
    }


def _stream_layer(x, pw, layer, kv_bufs, past_k, past_v, s0, tm, tf, tg, tq, n_sub):
    bsz, t, _ = x.shape
    x2d = x.reshape(bsz * t, D_MODEL)
    gq, gk, la, gv, gate, sq, skb, svb, skf, svf = _project(x2d, bsz, pw, layer, kv_bufs, tm)
    r3 = lambda a: a.reshape(bsz, t, a.shape[-1])
    o_gla, s_new = _gla(r3(gq), r3(gk), r3(la), r3(gv), s0, layer, tg)
    o_sb = _sb(r3(sq), r3(skb), r3(svb), past_k, past_v, layer, tq, n_sub)
    y = _finish(x2d, o_gla.reshape(bsz * t, GLA_WIDTH), gate, o_sb.reshape(bsz * t, SB_WIDTH), pw, layer, tf)
    return y.reshape(bsz, t, D_MODEL), (skf, svf), s_new


def kernel(x_prompt, x_sample, cache_sb_k, cache_sb_v, state_gla, norm1_g, w_in, w_a2, b_a2, q_norm_g, k_norm_g,
           gla_norm_g, w_out, norm2_g, w_up, w_down):
    pw = _prepare_weights(norm1_g, w_in, w_a2, b_a2, q_norm_g, k_norm_g, gla_norm_g, w_out, norm2_g, w_up, w_down)
    nb, past = cache_sb_k.shape[1], cache_sb_k.shape[2]
    past_k = cache_sb_k.reshape(DEPTH, nb, past * SB_HEADS, SB_HD)
    past_v = cache_sb_v.reshape(DEPTH, nb, past * SB_HEADS, SB_HD)
    xp, xs = x_prompt, x_sample
    kv_p, kv_s, states_p, states_s = None, None, [], []
    for layer in range(DEPTH):
        xp, kv_p, sp = _stream_layer(xp, pw, layer, kv_p, None, None, None, tm=512, tf=256, tg=512, tq=128, n_sub=4)
        xs, kv_s, ss = _stream_layer(xs, pw, layer, kv_s, past_k, past_v, state_gla, tm=512, tf=256, tg=CHUNK,
                                     tq=CHUNK, n_sub=1)
        states_p.append(sp)
        states_s.append(ss)
    return (xp, xs, kv_p[0], kv_p[1], jnp.stack(states_p), kv_s[0], kv_s[1], jnp.stack(states_s))
```

```python
import functools

import jax
import jax.numpy as jnp
from jax import lax
from jax.experimental import pallas as pl
from jax.experimental.pallas import tpu as pltpu

D_MODEL = 1024
DEPTH = 2
CHUNK = 64
GLA_HEADS = 4
GLA_DK = 64
GLA_DV = 128
GLA_KEY = GLA_HEADS * GLA_DK
GLA_WIDTH = GLA_HEADS * GLA_DV
GATE_RANK = 16
GATE_NORM = 16.0
SB_HEADS = 8
SB_HD = 64
SB_WIDTH = SB_HEADS * SB_HD
D_FF = 4 * D_MODEL
EPS = 1e-6

LANES = 128
ROW_TILE = 16
SB_UNIT = 3
SB_ZERO_LOG = 110.0
LOG2E = 1.4426950408889634
VMEM_LIMIT = 56 * 1024 * 1024

_GATE_COLS = 2 * GLA_KEY + 2 * GLA_WIDTH
_COL_QK = (0, 2 * GLA_KEY)
_COL_GV = (2 * GLA_KEY, 2 * GLA_KEY + GLA_WIDTH)
_COL_GATE = (2 * GLA_KEY + GLA_WIDTH, _GATE_COLS)
_COL_ALR = (_GATE_COLS, _GATE_COLS + LANES)
GLA_COLS = _COL_ALR[1]
SB_COL0 = _GATE_COLS + GATE_RANK
_COL_SQ = (0, SB_WIDTH)
_COL_SK = (SB_WIDTH, 2 * SB_WIDTH)
_COL_SV = (2 * SB_WIDTH, 3 * SB_WIDTH)

F32 = jnp.float32
BF16 = jnp.bfloat16


def _dot(a, b):
    return jnp.dot(a, b, preferred_element_type=F32)


def _dot_nt(a, b):
    return lax.dot_general(a, b, (((1,), (1,)), ((), ())), preferred_element_type=F32)


def _iota(shape, dim):
    return lax.broadcasted_iota(jnp.int32, shape, dim)


def _split3(x):
    p1 = x.astype(BF16)
    r1 = x - p1.astype(F32)
    p2 = r1.astype(BF16)
    p3 = (r1 - p2.astype(F32)).astype(BF16)
    return p1, p2, p3


def _split2(x):
    p1 = x.astype(BF16)
    p2 = (x - p1.astype(F32)).astype(BF16)
    return p1, p2


def _softplus(z):
    return jnp.maximum(z, 0.0) + jnp.log(1.0 + jnp.exp(-jnp.abs(z)))


def _rmsnorm_rows(x, g):
    ms = jnp.mean(x * x, axis=-1, keepdims=True)
    return x * lax.rsqrt(ms + EPS) * g


def _project_kernel(*refs, stacks_prev):
    x_ref, g1_ref, wgla_ref, wsb_ref, wa2_ref, ba2_ref, qg_ref, kg_ref = refs[:8]
    gq_ref, gk_ref, la_ref, gv_ref, gate_ref, sq_ref, skb_ref, svb_ref, k4_ref, v4_ref = refs[8 + 2 * stacks_prev:]
    if stacks_prev:
        prev_k_ref, prev_v_ref = refs[8:10]
        k4_ref[0] = prev_k_ref[...]
        v4_ref[0] = prev_v_ref[...]
        k4_ref, v4_ref = k4_ref.at[1], v4_ref.at[1]
    xn = _rmsnorm_rows(x_ref[...], g1_ref[...]).astype(BF16)
    proj = lambda w_ref, cols: _dot(xn, w_ref[:, cols[0]:cols[1]])

    qk = proj(wgla_ref, _COL_QK)
    gv = proj(wgla_ref, _COL_GV)
    gate = proj(wgla_ref, _COL_GATE)
    alr = proj(wgla_ref, _COL_ALR)
    sq = proj(wsb_ref, _COL_SQ)
    sk = proj(wsb_ref, _COL_SK)
    sv = proj(wsb_ref, _COL_SV)
    y = _dot(alr.astype(BF16), wa2_ref[...]) + ba2_ref[...]

    r = _iota((LANES, LANES), 0) // SB_HD
    c = _iota((LANES, LANES), 1) // SB_HD
    ones_bd = jnp.where(r == c, 1.0, 0.0).astype(BF16)

    def head_mean_squares(s):
        out = []
        for j in range(SB_WIDTH // LANES):
            sj = s[:, j * LANES:(j + 1) * LANES]
            hi, lo = _split2(sj * sj)
            out.append((_dot(hi, ones_bd) + _dot(lo, ones_bd)) * (1.0 / SB_HD))
        return jnp.concatenate(out, axis=-1)

    def store_cache_layout(ref, a):
        rows = ref.reshape(a.shape[0] * SB_HEADS, SB_HD)
        for h in range(SB_HEADS):
            rows[pl.ds(h, a.shape[0], stride=SB_HEADS), :] = a[:, h * SB_HD:(h + 1) * SB_HD]

    ms_q = head_mean_squares(sq)
    ms_k = head_mean_squares(sk)

    gq_ref[...] = qk[:, :GLA_KEY] * (GLA_DK ** -0.5)
    gk_ref[...] = qk[:, GLA_KEY:]
    gv_ref[...] = gv.astype(BF16)
    gate_ref[...] = gate
    svb_ref[...] = sv.astype(BF16)
    store_cache_layout(v4_ref, sv)
    la_ref[...] = (jnp.minimum(y, 0.0) - jnp.log(1.0 + jnp.exp(-jnp.abs(y)))) * (1.0 / GATE_NORM)
    sq_ref[...] = (sq * lax.rsqrt(ms_q + EPS) * qg_ref[...] * (SB_HD ** -0.5)).astype(BF16)
    sk_n = sk * lax.rsqrt(ms_k + EPS) * kg_ref[...]
    skb_ref[...] = sk_n.astype(BF16)
    store_cache_layout(k4_ref, sk_n)


def _project(x2d, bsz, pw, layer, prev_kv, tm):
    assert DEPTH == 2
    n = x2d.shape[0]
    t = n // bsz
    stacks_prev = prev_kv is not None
    row = lambda w: pl.BlockSpec((tm, w), lambda i: (i, 0))
    per_layer = lambda a: pl.BlockSpec((None,) + a.shape[1:], lambda i: (layer,) + (0,) * (a.ndim - 1))
    weights = (pw['g1'], pw['win'], pw['wsb'], pw['wa2'], pw['ba2'], pw['qg'], pw['kg'])
    weight_specs = [per_layer(w) for w in weights]
    weight_specs[1] = pl.BlockSpec((None, D_MODEL, GLA_COLS), lambda i: (layer, 0, 0))
    out_cols = ((GLA_KEY, F32), (GLA_KEY, F32), (GLA_KEY, F32), (GLA_WIDTH, BF16), (GLA_WIDTH, F32),
                (SB_WIDTH, BF16), (SB_WIDTH, BF16), (SB_WIDTH, BF16))
    if tm <= t:
        kv_block, kv_index = (1, tm, SB_HEADS, SB_HD), lambda i: (i // (t // tm), i % (t // tm), 0, 0)
    else:
        kv_block, kv_index = (tm // t, t, SB_HEADS, SB_HD), lambda i: (i, 0, 0, 0)
    kv_shape = (bsz, t, SB_HEADS, SB_HD)
    prev_spec = pl.BlockSpec(kv_block, kv_index)
    if stacks_prev:
        kv_spec = pl.BlockSpec((DEPTH,) + kv_block, lambda i: (0,) + kv_index(i))
        kv_shape = (DEPTH,) + kv_shape
    else:
        kv_spec = prev_spec
    kv_struct = jax.ShapeDtypeStruct(kv_shape, F32)
    return pl.pallas_call(
        functools.partial(_project_kernel, stacks_prev=stacks_prev),
        grid=(n // tm,),
        in_specs=[row(D_MODEL)] + weight_specs + ([prev_spec] * 2 if stacks_prev else []),
        out_specs=[row(w) for w, _ in out_cols] + [kv_spec, kv_spec],
        out_shape=[jax.ShapeDtypeStruct((n, w), dt) for w, dt in out_cols] + [kv_struct, kv_struct],
        compiler_params=pltpu.CompilerParams(dimension_semantics=("parallel",), vmem_limit_bytes=VMEM_LIMIT),
        name="project",
    )(x2d, *weights, *(prev_kv if stacks_prev else ()))


def _gla_kernel(*refs, n_chunks, chunks_per_step, has_init):
    if has_init:
        q_ref, k_ref, la_ref, v_ref, s0_ref, o_ref, sout_ref, s_scr, b_scr = refs
    else:
        q_ref, k_ref, la_ref, v_ref, o_ref, sout_ref, s_scr, b_scr = refs
    g = pl.program_id(2)

    @pl.when(g == 0)
    def _():
        if has_init:
            s_scr[...] = s0_ref[...].reshape(2 * GLA_DK, GLA_DV)
        else:
            s_scr[...] = jnp.zeros_like(s_scr)

    rr = _iota((CHUNK, CHUNK), 0)
    cc = _iota((CHUNK, CHUNK), 1)
    tri = jnp.where(cc <= rr, 1.0, 0.0).astype(BF16)
    r = _iota((2 * LANES, 2 * LANES), 0) // GLA_DK
    c = _iota((2 * LANES, 2 * LANES), 1) // GLA_DK
    ones_bd2 = jnp.where(r == c, 1.0, 0.0).astype(BF16)
    lane = _iota((CHUNK, LANES), 1)
    lane_t = lane % CHUNK
    tile_t = _iota((ROW_TILE, LANES), 1) % CHUNK
    srow = _iota((CHUNK, LANES), 0)
    head_a = lane < GLA_DK

    n_sub = CHUNK // ROW_TILE
    sub = lambda g: slice(g * ROW_TILE, (g + 1) * ROW_TILE)

    def operands(slot, base, b):
        b2 = b * LOG2E
        b_scr[slot] = b2
        rows = pl.ds(base, CHUNK)
        q = q_ref[rows, :]
        k = k_ref[rows, :]

        b_end = jnp.concatenate(
            [jnp.broadcast_to(b[(g + 1) * ROW_TILE - 1:(g + 1) * ROW_TILE], (ROW_TILE, LANES)) for g in range(n_sub)],
            axis=0)
        k_dec = (k * jnp.exp(b_end - b)).astype(BF16)
        q_dec_rhs = []
        for g in range(n_sub - 1):
            lo = (g + 1) * ROW_TILE
            q_dec = q[lo:] * jnp.exp(b[lo:] - b[lo - 1:lo])
            pad = jnp.zeros((lo, LANES), F32)
            in_a = _iota((CHUNK - lo, LANES), 1) < GLA_DK
            q_dec_rhs.append(
                jnp.concatenate([pad, jnp.where(in_a, q_dec, 0.0), pad, jnp.where(in_a, 0.0, q_dec)],
                                axis=0).astype(BF16))

        es = []
        for t in range(CHUNK):
            g = t // ROW_TILE
            brow = b_scr[slot, t:t + 1, :]
            qrow = q_ref[pl.ds(base + t, 1), :]
            es.append(jnp.exp2(brow - b2[sub(g)]).astype(BF16) * (qrow * k[sub(g)]).astype(BF16))
        diag_lhs = jnp.concatenate(
            [jnp.concatenate([es[2 * p], es[2 * p + 1]], axis=1) for p in range(CHUNK // 2)], axis=0)

        qe = q * jnp.exp(b)
        qe_a = jnp.where(head_a, qe, 0.0).astype(BF16)
        qe_b = jnp.where(head_a, 0.0, qe).astype(BF16)
        b_t = b.T
        b_last = b_t[:, CHUNK - 1:CHUNK]
        kd_t = (k.T * jnp.exp(b_last - b_t)).astype(BF16)
        return k_dec, q_dec_rhs, diag_lhs, qe_a, qe_b, kd_t, jnp.exp(b_last)

    def scores(off_tiles, red):
        pt_tiles = list(off_tiles) + [jnp.zeros((ROW_TILE, LANES), F32)]
        for t in range(CHUNK):
            g, p, half = t // ROW_TILE, t // 2, t % 2
            r_t = red[p * ROW_TILE:(p + 1) * ROW_TILE, half * LANES:(half + 1) * LANES]
            pt_tiles[g] = jnp.where(tile_t == t, r_t, pt_tiles[g])
        pt = jnp.concatenate(pt_tiles, axis=0)
        return jnp.where(srow <= lane_t, pt, 0.0)

    def step_body(si, carry):
        slots = range(chunks_per_step)
        bases = [pl.multiple_of((si * chunks_per_step + slot) * CHUNK, CHUNK) for slot in slots]
        bs = []
        for base in bases:
            p1, p2, p3 = _split3(la_ref[pl.ds(base, CHUNK), :])
            bs.append(_dot(tri, p1) + _dot(tri, p2) + _dot(tri, p3))
        ops = [operands(slot, bases[slot], bs[slot]) for slot in slots]
        v2s = [v_ref[pl.ds(base, CHUNK), :] for base in bases]
        offs = [[_dot_nt(op[0][sub(g)], op[1][g]) for g in range(n_sub - 1)] for op in ops]
        reds = [_dot(op[2], ones_bd2) for op in ops]
        upds = [_dot(op[5], v2) for op, v2 in zip(ops, v2s)]
        pts = [scores(off, red).T.astype(BF16) for off, red in zip(offs, reds)]
        o_intras = [_dot(pt, v2) for pt, v2 in zip(pts, v2s)]

        s = s_scr[...]
        for slot in slots:
            _, _, _, qe_a, qe_b, _, decay = ops[slot]
            rows = pl.ds(bases[slot], CHUNK)
            s_bf = s.astype(BF16)
            o_ref[rows, :GLA_DV] = o_intras[slot][:CHUNK, :GLA_DV] + _dot(qe_a, s_bf)
            o_ref[rows, GLA_DV:] = o_intras[slot][CHUNK:, GLA_DV:] + _dot(qe_b, s_bf)
            upd = upds[slot]
            s = decay * s + jnp.concatenate([upd[:GLA_DK, :GLA_DV], upd[GLA_DK:, GLA_DV:]], axis=0)
        s_scr[...] = s
        return carry

    lax.fori_loop(0, n_chunks // chunks_per_step, step_body, 0)

    @pl.when(g == pl.num_programs(2) - 1)
    def _():
        sout_ref[...] = s_scr[...].reshape(2, GLA_DK, GLA_DV)


def _gla(gq, gk, la, gv, s0, layer, tg):
    bsz, t, _ = gq.shape
    has_init = s0 is not None
    n_chunks = tg // CHUNK
    chunks_per_step = 8 if n_chunks % 8 == 0 else 1
    qspec = pl.BlockSpec((None, tg, LANES), lambda b, p, g: (b, g, p))
    vspec = pl.BlockSpec((None, tg, 2 * GLA_DV), lambda b, p, g: (b, g, p))
    sspec = pl.BlockSpec((None, 2, GLA_DK, GLA_DV), lambda b, p, g: (b, p, 0, 0))
    s0spec = pl.BlockSpec((None, None, 2, GLA_DK, GLA_DV), lambda b, p, g: (layer, b, p, 0, 0))
    in_specs = [qspec, qspec, qspec, vspec] + ([s0spec] if has_init else [])
    args = (gq, gk, la, gv) + ((s0,) if has_init else ())
    return pl.pallas_call(
        functools.partial(_gla_kernel, n_chunks=n_chunks, chunks_per_step=chunks_per_step, has_init=has_init),
        grid=(bsz, GLA_HEADS // 2, t // tg),
        in_specs=in_specs,
        out_specs=[vspec, sspec],
        out_shape=[jax.ShapeDtypeStruct((bsz, t, GLA_WIDTH), F32),
                   jax.ShapeDtypeStruct((bsz, GLA_HEADS, GLA_DK, GLA_DV), F32)],
        scratch_shapes=[pltpu.VMEM((2 * GLA_DK, GLA_DV), F32), pltpu.VMEM((chunks_per_step, CHUNK, LANES), F32)],
        compiler_params=pltpu.CompilerParams(
            dimension_semantics=("parallel", "parallel", "arbitrary"), vmem_limit_bytes=VMEM_LIMIT),
        name="gla",
    )(*args)


def _cumsum_matrix(tk):
    r = _iota((tk, 2 * tk), 0)
    c = _iota((tk, 2 * tk), 1)
    return jnp.where((c >= tk) | (r > c), 1.0, 0.0).astype(BF16)


def _sb_unit(streams):
    zs = [[_dot_nt(q2, kj) for kj, _, _, _ in blocks] for q2, blocks, _, _ in streams]
    sps = []
    for z_list, (_, blocks, _, _) in zip(zs, streams):
        sp_list = []
        for z, (_, _, _, mask) in zip(z_list, blocks):
            sp = _softplus(z)
            sp_list.append(sp if mask is None else jnp.where(mask, sp, 0.0))
        sps.append(sp_list)
    crs = []
    for sp_list, (q2, blocks, _, _) in zip(sps, streams):
        cr_list = [None] * len(blocks)
        rows = q2.shape[0]
        for tk in sorted({b[0].shape[0] for b in blocks}):
            idx = [n for n, b in enumerate(blocks) if b[0].shape[0] == tk]
            u = blocks[idx[0]][2]
            cr = _dot(jnp.concatenate([sp_list[n] for n in idx], axis=0).astype(BF16), u)
            for m, n in enumerate(idx):
                cr_list[n] = cr[m * rows:(m + 1) * rows]
        crs.append(cr_list)
    ws = []
    carries = []
    for z_list, sp_list, cr_list, (_, blocks, carry, _) in zip(zs, sps, crs, streams):
        w_list = []
        for z, sp, cr, (kj, _, _, mask) in zip(z_list, sp_list, cr_list, blocks):
            tk = kj.shape[0]
            w = jnp.exp(z - sp - cr[:, :tk] - carry[:, :tk])
            if mask is not None:
                w = jnp.where(mask, w, 0.0)
            w_list.append(w.astype(BF16))
            rs = cr[:, tk:]
            if tk < LANES:
                rs = jnp.concatenate([rs] * (LANES // tk), axis=1)
            carry = carry + rs
        ws.append(w_list)
        carries.append(carry)
    out = []
    for w_list, carry, (_, blocks, _, acc) in zip(ws, carries, streams):
        pv = None
        for w, (_, vj, _, _) in zip(w_list, blocks):
            d = _dot(w, vj)
            pv = d if pv is None else pv + d
        out.append((carry, acc + pv))
    return out


def _sb_kernel(*refs, tq, n_sub, n_past_blocks):
    if n_past_blocks:
        q_ref, k_ref, v_ref, pk_ref, pv_ref, o_ref = refs
    else:
        q_ref, k_ref, v_ref, o_ref = refs
    i = pl.program_id(2)
    lane = _iota((tq, LANES), 1)
    u_own = _cumsum_matrix(tq)
    strict = _iota((2 * tq, tq), 1) < _iota((2 * tq, tq), 0) % tq
    zeros = jnp.zeros((2 * tq, LANES), F32)

    def stacked_heads(r):
        q = q_ref[r * tq:(r + 1) * tq, :]
        zero_q = jnp.zeros_like(q)
        return jnp.concatenate([jnp.where(lane < SB_HD, q, zero_q), jnp.where(lane < SB_HD, zero_q, q)], axis=0)

    q2s = [stacked_heads(r) for r in range(n_sub)]

    if n_past_blocks:
        u_past = _cumsum_matrix(LANES)
        pair = pl.program_id(1)

        def past_block(ref, j):
            halves = [ref[pl.ds(j * LANES * SB_HEADS + 2 * pair + hh, LANES, stride=SB_HEADS), :] for hh in range(2)]
            return jnp.concatenate(halves, axis=1).astype(BF16)

        blocks = [(k_ref[...], v_ref[...], u_own, strict)]
        for j in reversed(range(n_past_blocks)):
            blocks.append((past_block(pk_ref, j), past_block(pv_ref, j), u_past, None))
        accs = [_sb_unit([(q2s[0], blocks, zeros, zeros)])[0][1]]
    else:
        def load_unit(r, n, first):
            blocks = []
            for kk in range(SB_UNIT):
                j = i * n_sub + r - n * SB_UNIT - kk
                rows = pl.ds(pl.multiple_of(jnp.maximum(j, 0) * tq, tq), tq)
                vj = v_ref[rows, :]
                diag = first and kk == 0
                if not diag:
                    vj = jnp.where(j >= 0, vj, jnp.zeros_like(vj))
                blocks.append((k_ref[rows, :], vj, u_own, strict if diag else None))
            return blocks

        def min_carry(state):
            m = state[0][0]
            for carry, _ in state[1:]:
                m = jnp.minimum(m, carry)
            return jnp.min(m)

        state = _sb_unit([(q2s[r], load_unit(r, 0, True), zeros, zeros) for r in range(n_sub)])
        n_units = (i * n_sub + n_sub - 1 + SB_UNIT) // SB_UNIT

        def cond(st):
            return (st[0] < n_units) & (st[1] < SB_ZERO_LOG)

        def body(st):
            n, _, state = st
            state = _sb_unit([(q2s[r], load_unit(r, n, False), *state[r]) for r in range(n_sub)])
            return n + 1, min_carry(state), state

        state = lax.while_loop(cond, body, (jnp.int32(1), min_carry(state), state))[2]
        accs = [acc for _, acc in state]
    for r, acc in enumerate(accs):
        o_ref[r * tq:(r + 1) * tq, :] = jnp.where(lane < SB_HD, acc[:tq], acc[tq:]).astype(o_ref.dtype)


def _sb(q, k, v, past_k, past_v, layer, tq, n_sub):
    bsz, t, _ = q.shape
    n_past_blocks = 0 if past_k is None else past_k.shape[2] // (LANES * SB_HEADS)
    qspec = pl.BlockSpec((None, tq * n_sub, LANES), lambda b, p, i: (b, i, p))
    kspec = pl.BlockSpec((None, t, LANES), lambda b, p, i: (b, 0, p))
    in_specs = [qspec, kspec, kspec]
    args = (q, k, v)
    if n_past_blocks:
        pspec = pl.BlockSpec((None, None) + past_k.shape[2:], lambda b, p, i: (layer, b, 0, 0))
        in_specs += [pspec, pspec]
        args += (past_k, past_v)
    return pl.pallas_call(
        functools.partial(_sb_kernel, tq=tq, n_sub=n_sub, n_past_blocks=n_past_blocks),
        grid=(bsz, SB_WIDTH // LANES, t // (tq * n_sub)),
        in_specs=in_specs,
        out_specs=qspec,
        out_shape=jax.ShapeDtypeStruct((bsz, t, SB_WIDTH), BF16),
        compiler_params=pltpu.CompilerParams(
            dimension_semantics=("parallel", "parallel", "arbitrary"), vmem_limit_bytes=VMEM_LIMIT),
        name="sb",
    )(*args)


def _finish_kernel(x_ref, og_ref, gate_ref, osb_ref, gg_ref, wo_ref, g2_ref, wup_ref, wdn_ref, y_ref, *, ff_block):
    og = og_ref[...]
    normed = [_rmsnorm_rows(og[:, h * GLA_DV:(h + 1) * GLA_DV], gg_ref[...]) for h in range(GLA_HEADS)]
    gate = gate_ref[...]
    mix_g = (jnp.concatenate(normed, axis=-1) * (gate * jax.nn.sigmoid(gate))).astype(BF16)
    h = x_ref[...] + _dot(mix_g, wo_ref[:GLA_WIDTH, :]) + _dot(osb_ref[...], wo_ref[GLA_WIDTH:, :])
    hn = _rmsnorm_rows(h, g2_ref[...]).astype(BF16)
    mlp = None
    for j in range(D_FF // ff_block):
        cols = slice(j * ff_block, (j + 1) * ff_block)
        u = _dot(hn, wup_ref[:, cols])
        a = jnp.square(jnp.maximum(u, 0.0)).astype(BF16)
        d = _dot(a, wdn_ref[cols, :])
        mlp = d if mlp is None else mlp + d
    y_ref[...] = h + mlp


def _finish(x2d, og, gate, osb, pw, layer, tm):
    n = x2d.shape[0]
    row = lambda w: pl.BlockSpec((tm, w), lambda i: (i, 0))
    per_layer = lambda a: pl.BlockSpec((None,) + a.shape[1:], lambda i: (layer,) + (0,) * (a.ndim - 1))
    weights = (pw['gg'], pw['wo'], pw['g2'], pw['wup'], pw['wdn'])
    return pl.pallas_call(
        functools.partial(_finish_kernel, ff_block=1024),
        grid=(n // tm,),
        in_specs=[row(D_MODEL), row(GLA_WIDTH), row(GLA_WIDTH), row(SB_WIDTH)] + [per_layer(w) for w in weights],
        out_specs=row(D_MODEL),
        out_shape=jax.ShapeDtypeStruct((n, D_MODEL), F32),
        compiler_params=pltpu.CompilerParams(dimension_semantics=("parallel",), vmem_limit_bytes=VMEM_LIMIT),
        name="finish",
    )(x2d, og, gate, osb, *weights)


def _prepare_weights(norm1_g, w_in, w_a2, b_a2, q_norm_g, k_norm_g, gla_norm_g, w_out, norm2_g, w_up, w_down):
    w_in = w_in.astype(BF16)
    row = lambda a: a.reshape(DEPTH, 1, a.shape[-1])
    return {
        'g1': row(norm1_g),
        'win': w_in,
        'wsb': w_in[:, :, SB_COL0:],
        'wa2': jnp.pad(w_a2, ((0, 0), (0, LANES - GATE_RANK), (0, 0))).astype(BF16),
        'ba2': row(b_a2),
        'qg': row(jnp.tile(q_norm_g, (1, SB_HEADS))),
        'kg': row(jnp.tile(k_norm_g, (1, SB_HEADS))),
        'gg': row(gla_norm_g),
        'wo': w_out.astype(BF16),
        'g2': row(norm2_g),
        'wup': w_up.astype(BF16),
        'wdn': w_down.astype(BF16),
    }


def _stream_layer(x, pw, layer, prev_kv, past_k, past_v, s0, tm, tf, tg, tq, n_sub):
    bsz, t, _ = x.shape
    x2d = x.reshape(bsz * t, D_MODEL)
    gq, gk, la, gv, gate, sq, skb, svb, skf, svf = _project(x2d, bsz, pw, layer, prev_kv, tm)
    r3 = lambda a: a.reshape(bsz, t, a.shape[-1])
    o_gla, s_new = _gla(r3(gq), r3(gk), r3(la), r3(gv), s0, layer, tg)
    o_sb = _sb(r3(sq), r3(skb), r3(svb), past_k, past_v, layer, tq, n_sub)
    y = _finish(x2d, o_gla.reshape(bsz * t, GLA_WIDTH), gate, o_sb.reshape(bsz * t, SB_WIDTH), pw, layer, tf)
    return y.reshape(bsz, t, D_MODEL), (skf, svf), s_new


def kernel(x_prompt, x_sample, cache_sb_k, cache_sb_v, state_gla, norm1_g, w_in, w_a2, b_a2, q_norm_g, k_norm_g,
           gla_norm_g, w_out, norm2_g, w_up, w_down):
    pw = _prepare_weights(norm1_g, w_in, w_a2, b_a2, q_norm_g, k_norm_g, gla_norm_g, w_out, norm2_g, w_up, w_down)
    nb, past = cache_sb_k.shape[1], cache_sb_k.shape[2]
    past_k = cache_sb_k.reshape(DEPTH, nb, past * SB_HEADS, SB_HD)
    past_v = cache_sb_v.reshape(DEPTH, nb, past * SB_HEADS, SB_HD)
    xp, xs = x_prompt, x_sample
    kv_p, kv_s, states_p, states_s = None, None, [], []
    for layer in range(DEPTH):
        xp, kv_p, sp = _stream_layer(xp, pw, layer, kv_p, None, None, None, tm=512 if layer == 0 else 256, tf=256,
                                     tg=512, tq=128, n_sub=4)
        xs, kv_s, ss = _stream_layer(xs, pw, layer, kv_s, past_k, past_v, state_gla, tm=512, tf=256, tg=CHUNK,
                                     tq=CHUNK, n_sub=1)
        states_p.append(sp)
        states_s.append(ss)
    return (xp, xs, kv_p[0], kv_p[1], jnp.stack(states_p), kv_s[0], kv_s[1], jnp.stack(states_s))
```

```python
import functools

import jax
import jax.numpy as jnp
from jax import lax
from jax.experimental import pallas as pl
from jax.experimental.pallas import tpu as pltpu

D_MODEL = 1024
DEPTH = 2
CHUNK = 64
GLA_HEADS = 4
GLA_DK = 64
GLA_DV = 128
GLA_KEY = GLA_HEADS * GLA_DK
GLA_WIDTH = GLA_HEADS * GLA_DV
GATE_RANK = 16
GATE_NORM = 16.0
SB_HEADS = 8
SB_HD = 64
SB_WIDTH = SB_HEADS * SB_HD
D_FF = 4 * D_MODEL
EPS = 1e-6

LANES = 128
ROW_TILE = 16
SB_UNIT = 3
SB_ZERO_LOG = 110.0
LOG2E = 1.4426950408889634
VMEM_LIMIT = 56 * 1024 * 1024

_GATE_COLS = 2 * GLA_KEY + 2 * GLA_WIDTH
_COL_QK = (0, 2 * GLA_KEY)
_COL_GV = (2 * GLA_KEY, 2 * GLA_KEY + GLA_WIDTH)
_COL_GATE = (2 * GLA_KEY + GLA_WIDTH, _GATE_COLS)
_COL_ALR = (_GATE_COLS, _GATE_COLS + LANES)
GLA_COLS = _COL_ALR[1]
SB_COL0 = _GATE_COLS + GATE_RANK
_COL_SQ = (0, SB_WIDTH)
_COL_SK = (SB_WIDTH, 2 * SB_WIDTH)
_COL_SV = (2 * SB_WIDTH, 3 * SB_WIDTH)

F32 = jnp.float32
BF16 = jnp.bfloat16


def _dot(a, b):
    return jnp.dot(a, b, preferred_element_type=F32)


def _dot_nt(a, b):
    return lax.dot_general(a, b, (((1,), (1,)), ((), ())), preferred_element_type=F32)


def _iota(shape, dim):
    return lax.broadcasted_iota(jnp.int32, shape, dim)


def _split3(x):
    p1 = x.astype(BF16)
    r1 = x - p1.astype(F32)
    p2 = r1.astype(BF16)
    p3 = (r1 - p2.astype(F32)).astype(BF16)
    return p1, p2, p3


def _split2(x):
    p1 = x.astype(BF16)
    p2 = (x - p1.astype(F32)).astype(BF16)
    return p1, p2


def _softplus(z):
    return jnp.maximum(z, 0.0) + jnp.log(1.0 + jnp.exp(-jnp.abs(z)))


def _rmsnorm_rows(x, g):
    ms = jnp.mean(x * x, axis=-1, keepdims=True)
    return x * lax.rsqrt(ms + EPS) * g


def _project_kernel(*refs, aliased):
    x_ref, g1_ref, wgla_ref, wsb_ref, wa2_ref, ba2_ref, qg_ref, kg_ref = refs[:8]
    gq_ref, gk_ref, la_ref, gv_ref, gate_ref, sq_ref, skb_ref, svb_ref, k4_ref, v4_ref = refs[8 + 2 * aliased:]
    xn = _rmsnorm_rows(x_ref[...], g1_ref[...]).astype(BF16)
    proj = lambda w_ref, cols: _dot(xn, w_ref[:, cols[0]:cols[1]])

    qk = proj(wgla_ref, _COL_QK)
    gv = proj(wgla_ref, _COL_GV)
    gate = proj(wgla_ref, _COL_GATE)
    alr = proj(wgla_ref, _COL_ALR)
    sq = proj(wsb_ref, _COL_SQ)
    sk = proj(wsb_ref, _COL_SK)
    sv = proj(wsb_ref, _COL_SV)
    y = _dot(alr.astype(BF16), wa2_ref[...]) + ba2_ref[...]

    r = _iota((LANES, LANES), 0) // SB_HD
    c = _iota((LANES, LANES), 1) // SB_HD
    ones_bd = jnp.where(r == c, 1.0, 0.0).astype(BF16)

    def head_mean_squares(s):
        out = []
        for j in range(SB_WIDTH // LANES):
            sj = s[:, j * LANES:(j + 1) * LANES]
            hi, lo = _split2(sj * sj)
            out.append((_dot(hi, ones_bd) + _dot(lo, ones_bd)) * (1.0 / SB_HD))
        return jnp.concatenate(out, axis=-1)

    def store_cache_layout(ref, a):
        rows = ref.reshape(a.shape[0] * SB_HEADS, SB_HD)
        for h in range(SB_HEADS):
            rows[pl.ds(h, a.shape[0], stride=SB_HEADS), :] = a[:, h * SB_HD:(h + 1) * SB_HD]

    ms_q = head_mean_squares(sq)
    ms_k = head_mean_squares(sk)

    gq_ref[...] = qk[:, :GLA_KEY] * (GLA_DK ** -0.5)
    gk_ref[...] = qk[:, GLA_KEY:]
    gv_ref[...] = gv.astype(BF16)
    gate_ref[...] = gate
    svb_ref[...] = sv.astype(BF16)
    store_cache_layout(v4_ref, sv)
    la_ref[...] = (jnp.minimum(y, 0.0) - jnp.log(1.0 + jnp.exp(-jnp.abs(y)))) * (1.0 / GATE_NORM)
    sq_ref[...] = (sq * lax.rsqrt(ms_q + EPS) * qg_ref[...] * (SB_HD ** -0.5)).astype(BF16)
    sk_n = sk * lax.rsqrt(ms_k + EPS) * kg_ref[...]
    skb_ref[...] = sk_n.astype(BF16)
    store_cache_layout(k4_ref, sk_n)


def _project(x2d, bsz, pw, layer, kv_bufs, tm):
    n = x2d.shape[0]
    t = n // bsz
    aliased = kv_bufs is not None
    row = lambda w: pl.BlockSpec((tm, w), lambda i: (i, 0))
    per_layer = lambda a: pl.BlockSpec((None,) + a.shape[1:], lambda i: (layer,) + (0,) * (a.ndim - 1))
    weights = (pw['g1'], pw['win'], pw['wsb'], pw['wa2'], pw['ba2'], pw['qg'], pw['kg'])
    weight_specs = [per_layer(w) for w in weights]
    weight_specs[1] = pl.BlockSpec((None, D_MODEL, GLA_COLS), lambda i: (layer, 0, 0))
    out_cols = ((GLA_KEY, F32), (GLA_KEY, F32), (GLA_KEY, F32), (GLA_WIDTH, BF16), (GLA_WIDTH, F32),
                (SB_WIDTH, BF16), (SB_WIDTH, BF16), (SB_WIDTH, BF16))
    if tm <= t:
        kv_spec = pl.BlockSpec((None, 1, tm, SB_HEADS, SB_HD), lambda i: (layer, i // (t // tm), i % (t // tm), 0, 0))
    else:
        kv_spec = pl.BlockSpec((None, tm // t, t, SB_HEADS, SB_HD), lambda i: (layer, i, 0, 0, 0))
    kv_shape = jax.ShapeDtypeStruct((DEPTH, bsz, t, SB_HEADS, SB_HD), F32)
    n_in = 1 + len(weights)
    return pl.pallas_call(
        functools.partial(_project_kernel, aliased=aliased),
        grid=(n // tm,),
        in_specs=[row(D_MODEL)] + weight_specs + ([pl.BlockSpec(memory_space=pl.ANY)] * 2 if aliased else []),
        out_specs=[row(w) for w, _ in out_cols] + [kv_spec, kv_spec],
        out_shape=[jax.ShapeDtypeStruct((n, w), dt) for w, dt in out_cols] + [kv_shape, kv_shape],
        input_output_aliases={n_in: len(out_cols), n_in + 1: len(out_cols) + 1} if aliased else {},
        compiler_params=pltpu.CompilerParams(dimension_semantics=("parallel",), vmem_limit_bytes=VMEM_LIMIT),
        name="project",
    )(x2d, *weights, *(kv_bufs if aliased else ()))


def _gla_kernel(*refs, n_chunks, chunks_per_step, has_init):
    if has_init:
        q_ref, k_ref, la_ref, v_ref, s0_ref, o_ref, sout_ref, s_scr, b_scr = refs
    else:
        q_ref, k_ref, la_ref, v_ref, o_ref, sout_ref, s_scr, b_scr = refs
    g = pl.program_id(2)

    @pl.when(g == 0)
    def _():
        if has_init:
            s_scr[...] = s0_ref[...].reshape(2 * GLA_DK, GLA_DV)
        else:
            s_scr[...] = jnp.zeros_like(s_scr)

    rr = _iota((CHUNK, CHUNK), 0)
    cc = _iota((CHUNK, CHUNK), 1)
    tri = jnp.where(cc <= rr, 1.0, 0.0).astype(BF16)
    r = _iota((2 * LANES, 2 * LANES), 0) // GLA_DK
    c = _iota((2 * LANES, 2 * LANES), 1) // GLA_DK
    ones_bd2 = jnp.where(r == c, 1.0, 0.0).astype(BF16)
    lane = _iota((CHUNK, LANES), 1)
    lane_t = lane % CHUNK
    tile_t = _iota((ROW_TILE, LANES), 1) % CHUNK
    srow = _iota((CHUNK, LANES), 0)
    head_a = lane < GLA_DK

    n_sub = CHUNK // ROW_TILE
    sub = lambda g: slice(g * ROW_TILE, (g + 1) * ROW_TILE)

    def operands(slot, base, b):
        b2 = b * LOG2E
        b_scr[slot] = b2
        rows = pl.ds(base, CHUNK)
        q = q_ref[rows, :]
        k = k_ref[rows, :]

        b_end = jnp.concatenate(
            [jnp.broadcast_to(b[(g + 1) * ROW_TILE - 1:(g + 1) * ROW_TILE], (ROW_TILE, LANES)) for g in range(n_sub)],
            axis=0)
        k_dec = (k * jnp.exp(b_end - b)).astype(BF16)
        q_dec_rhs = []
        for g in range(n_sub - 1):
            lo = (g + 1) * ROW_TILE
            q_dec = q[lo:] * jnp.exp(b[lo:] - b[lo - 1:lo])
            pad = jnp.zeros((lo, LANES), F32)
            in_a = _iota((CHUNK - lo, LANES), 1) < GLA_DK
            q_dec_rhs.append(
                jnp.concatenate([pad, jnp.where(in_a, q_dec, 0.0), pad, jnp.where(in_a, 0.0, q_dec)],
                                axis=0).astype(BF16))

        es = []
        for t in range(CHUNK):
            g = t // ROW_TILE
            brow = b_scr[slot, t:t + 1, :]
            qrow = q_ref[pl.ds(base + t, 1), :]
            es.append(jnp.exp2(brow - b2[sub(g)]).astype(BF16) * (qrow * k[sub(g)]).astype(BF16))
        diag_lhs = jnp.concatenate(
            [jnp.concatenate([es[2 * p], es[2 * p + 1]], axis=1) for p in range(CHUNK // 2)], axis=0)

        qe = q * jnp.exp(b)
        qe_a = jnp.where(head_a, qe, 0.0).astype(BF16)
        qe_b = jnp.where(head_a, 0.0, qe).astype(BF16)
        b_t = b.T
        b_last = b_t[:, CHUNK - 1:CHUNK]
        kd_t = (k.T * jnp.exp(b_last - b_t)).astype(BF16)
        return k_dec, q_dec_rhs, diag_lhs, qe_a, qe_b, kd_t, jnp.exp(b_last)

    def scores(off_tiles, red):
        pt_tiles = list(off_tiles) + [jnp.zeros((ROW_TILE, LANES), F32)]
        for t in range(CHUNK):
            g, p, half = t // ROW_TILE, t // 2, t % 2
            r_t = red[p * ROW_TILE:(p + 1) * ROW_TILE, half * LANES:(half + 1) * LANES]
            pt_tiles[g] = jnp.where(tile_t == t, r_t, pt_tiles[g])
        pt = jnp.concatenate(pt_tiles, axis=0)
        return jnp.where(srow <= lane_t, pt, 0.0)

    def step_body(si, carry):
        slots = range(chunks_per_step)
        bases = [pl.multiple_of((si * chunks_per_step + slot) * CHUNK, CHUNK) for slot in slots]
        bs = []
        for base in bases:
            p1, p2, p3 = _split3(la_ref[pl.ds(base, CHUNK), :])
            bs.append(_dot(tri, p1) + _dot(tri, p2) + _dot(tri, p3))
        ops = [operands(slot, bases[slot], bs[slot]) for slot in slots]
        v2s = [v_ref[pl.ds(base, CHUNK), :] for base in bases]
        offs = [[_dot_nt(op[0][sub(g)], op[1][g]) for g in range(n_sub - 1)] for op in ops]
        reds = [_dot(op[2], ones_bd2) for op in ops]
        upds = [_dot(op[5], v2) for op, v2 in zip(ops, v2s)]
        pts = [scores(off, red).T.astype(BF16) for off, red in zip(offs, reds)]
        o_intras = [_dot(pt, v2) for pt, v2 in zip(pts, v2s)]

        s = s_scr[...]
        for slot in slots:
            _, _, _, qe_a, qe_b, _, decay = ops[slot]
            rows = pl.ds(bases[slot], CHUNK)
            s_bf = s.astype(BF16)
            o_ref[rows, :GLA_DV] = o_intras[slot][:CHUNK, :GLA_DV] + _dot(qe_a, s_bf)
            o_ref[rows, GLA_DV:] = o_intras[slot][CHUNK:, GLA_DV:] + _dot(qe_b, s_bf)
            upd = upds[slot]
            s = decay * s + jnp.concatenate([upd[:GLA_DK, :GLA_DV], upd[GLA_DK:, GLA_DV:]], axis=0)
        s_scr[...] = s
        return carry

    lax.fori_loop(0, n_chunks // chunks_per_step, step_body, 0)

    @pl.when(g == pl.num_programs(2) - 1)
    def _():
        sout_ref[...] = s_scr[...].reshape(2, GLA_DK, GLA_DV)


def _gla(gq, gk, la, gv, s0, layer, tg):
    bsz, t, _ = gq.shape
    has_init = s0 is not None
    n_chunks = tg // CHUNK
    chunks_per_step = 8 if n_chunks % 8 == 0 else 1
    qspec = pl.BlockSpec((None, tg, LANES), lambda b, p, g: (b, g, p))
    vspec = pl.BlockSpec((None, tg, 2 * GLA_DV), lambda b, p, g: (b, g, p))
    sspec = pl.BlockSpec((None, 2, GLA_DK, GLA_DV), lambda b, p, g: (b, p, 0, 0))
    s0spec = pl.BlockSpec((None, None, 2, GLA_DK, GLA_DV), lambda b, p, g: (layer, b, p, 0, 0))
    in_specs = [qspec, qspec, qspec, vspec] + ([s0spec] if has_init else [])
    args = (gq, gk, la, gv) + ((s0,) if has_init else ())
    return pl.pallas_call(
        functools.partial(_gla_kernel, n_chunks=n_chunks, chunks_per_step=chunks_per_step, has_init=has_init),
        grid=(bsz, GLA_HEADS // 2, t // tg),
        in_specs=in_specs,
        out_specs=[vspec, sspec],
        out_shape=[jax.ShapeDtypeStruct((bsz, t, GLA_WIDTH), F32),
                   jax.ShapeDtypeStruct((bsz, GLA_HEADS, GLA_DK, GLA_DV), F32)],
        scratch_shapes=[pltpu.VMEM((2 * GLA_DK, GLA_DV), F32), pltpu.VMEM((chunks_per_step, CHUNK, LANES), F32)],
        compiler_params=pltpu.CompilerParams(
            dimension_semantics=("parallel", "parallel", "arbitrary"), vmem_limit_bytes=VMEM_LIMIT),
        name="gla",
    )(*args)


def _cumsum_matrix(tk):
    r = _iota((tk, 2 * tk), 0)
    c = _iota((tk, 2 * tk), 1)
    return jnp.where((c >= tk) | (r > c), 1.0, 0.0).astype(BF16)


def _sb_unit(streams):
    n_keys = lambda b: b[0].shape[1] if b[4] else b[0].shape[0]
    zs = [[_dot(q2, b[0]) if b[4] else _dot_nt(q2, b[0]) for b in blocks] for q2, blocks, _, _ in streams]
    sps = []
    for z_list, (_, blocks, _, _) in zip(zs, streams):
        sp_list = []
        for z, (_, _, _, mask, _) in zip(z_list, blocks):
            sp = _softplus(z)
            sp_list.append(sp if mask is None else jnp.where(mask, sp, 0.0))
        sps.append(sp_list)
    crs = []
    for sp_list, (q2, blocks, _, _) in zip(sps, streams):
        cr_list = [None] * len(blocks)
        rows = q2.shape[0]
        for tk in sorted({n_keys(b) for b in blocks}):
            idx = [n for n, b in enumerate(blocks) if n_keys(b) == tk]
            u = blocks[idx[0]][2]
            cr = _dot(jnp.concatenate([sp_list[n] for n in idx], axis=0).astype(BF16), u)
            for m, n in enumerate(idx):
                cr_list[n] = cr[m * rows:(m + 1) * rows]
        crs.append(cr_list)
    ws = []
    carries = []
    for z_list, sp_list, cr_list, (_, blocks, carry, _) in zip(zs, sps, crs, streams):
        w_list = []
        for z, sp, cr, block in zip(z_list, sp_list, cr_list, blocks):
            tk, mask = n_keys(block), block[3]
            w = jnp.exp(z - sp - cr[:, :tk] - carry[:, :tk])
            if mask is not None:
                w = jnp.where(mask, w, 0.0)
            w_list.append(w.astype(BF16))
            rs = cr[:, tk:]
            if tk < LANES:
                rs = jnp.concatenate([rs] * (LANES // tk), axis=1)
            carry = carry + rs
        ws.append(w_list)
        carries.append(carry)
    out = []
    for w_list, carry, (_, blocks, _, acc) in zip(ws, carries, streams):
        pv = None
        for w, (_, vj, _, _, transposed) in zip(w_list, blocks):
            d = _dot_nt(w, vj) if transposed else _dot(w, vj)
            pv = d if pv is None else pv + d
        out.append((carry, acc + pv))
    return out


def _sb_kernel(*refs, tq, n_sub, n_past_blocks):
    if n_past_blocks:
        q_ref, k_ref, v_ref, pk_ref, pv_ref, o_ref = refs
    else:
        q_ref, k_ref, v_ref, o_ref = refs
    i = pl.program_id(2)
    lane = _iota((tq, LANES), 1)
    u_own = _cumsum_matrix(tq)
    strict = _iota((2 * tq, tq), 1) < _iota((2 * tq, tq), 0) % tq
    zeros = jnp.zeros((2 * tq, LANES), F32)

    def stacked_heads(r):
        q = q_ref[r * tq:(r + 1) * tq, :]
        zero_q = jnp.zeros_like(q)
        return jnp.concatenate([jnp.where(lane < SB_HD, q, zero_q), jnp.where(lane < SB_HD, zero_q, q)], axis=0)

    q2s = [stacked_heads(r) for r in range(n_sub)]

    if n_past_blocks:
        u_past = _cumsum_matrix(LANES)

        def past_block(ref, j):
            return ref[:, :, j * LANES:(j + 1) * LANES].reshape(2 * SB_HD, LANES).astype(BF16)

        blocks = [(k_ref[...], v_ref[...], u_own, strict, False)]
        for j in reversed(range(n_past_blocks)):
            blocks.append((past_block(pk_ref, j), past_block(pv_ref, j), u_past, None, True))
        accs = [_sb_unit([(q2s[0], blocks, zeros, zeros)])[0][1]]
    else:
        def load_unit(r, n, first):
            blocks = []
            for kk in range(SB_UNIT):
                j = i * n_sub + r - n * SB_UNIT - kk
                rows = pl.ds(pl.multiple_of(jnp.maximum(j, 0) * tq, tq), tq)
                vj = v_ref[rows, :]
                diag = first and kk == 0
                if not diag:
                    vj = jnp.where(j >= 0, vj, jnp.zeros_like(vj))
                blocks.append((k_ref[rows, :], vj, u_own, strict if diag else None, False))
            return blocks

        def min_carry(state):
            m = state[0][0]
            for carry, _ in state[1:]:
                m = jnp.minimum(m, carry)
            return jnp.min(m)

        state = _sb_unit([(q2s[r], load_unit(r, 0, True), zeros, zeros) for r in range(n_sub)])
        n_units = (i * n_sub + n_sub - 1 + SB_UNIT) // SB_UNIT

        def cond(st):
            return (st[0] < n_units) & (st[1] < SB_ZERO_LOG)

        def body(st):
            n, _, state = st
            state = _sb_unit([(q2s[r], load_unit(r, n, False), *state[r]) for r in range(n_sub)])
            return n + 1, min_carry(state), state

        state = lax.while_loop(cond, body, (jnp.int32(1), min_carry(state), state))[2]
        accs = [acc for _, acc in state]
    for r, acc in enumerate(accs):
        o_ref[r * tq:(r + 1) * tq, :] = jnp.where(lane < SB_HD, acc[:tq], acc[tq:]).astype(o_ref.dtype)


def _sb(q, k, v, past_k, past_v, layer, tq, n_sub):
    bsz, t, _ = q.shape
    n_past_blocks = 0 if past_k is None else past_k.shape[-1] // LANES
    qspec = pl.BlockSpec((None, tq * n_sub, LANES), lambda b, p, i: (b, i, p))
    kspec = pl.BlockSpec((None, t, LANES), lambda b, p, i: (b, 0, p))
    in_specs = [qspec, kspec, kspec]
    args = (q, k, v)
    if n_past_blocks:
        pspec = pl.BlockSpec((None, None, 2, SB_HD, past_k.shape[-1]), lambda b, p, i: (layer, b, p, 0, 0))
        in_specs += [pspec, pspec]
        args += (past_k, past_v)
    return pl.pallas_call(
        functools.partial(_sb_kernel, tq=tq, n_sub=n_sub, n_past_blocks=n_past_blocks),
        grid=(bsz, SB_WIDTH // LANES, t // (tq * n_sub)),
        in_specs=in_specs,
        out_specs=qspec,
        out_shape=jax.ShapeDtypeStruct((bsz, t, SB_WIDTH), BF16),
        compiler_params=pltpu.CompilerParams(
            dimension_semantics=("parallel", "parallel", "arbitrary"), vmem_limit_bytes=VMEM_LIMIT),
        name="sb",
    )(*args)


def _finish_kernel(x_ref, og_ref, gate_ref, osb_ref, gg_ref, wo_ref, g2_ref, wup_ref, wdn_ref, y_ref, *, ff_block):
    og = og_ref[...]
    normed = [_rmsnorm_rows(og[:, h * GLA_DV:(h + 1) * GLA_DV], gg_ref[...]) for h in range(GLA_HEADS)]
    gate = gate_ref[...]
    mix_g = (jnp.concatenate(normed, axis=-1) * (gate * jax.nn.sigmoid(gate))).astype(BF16)
    h = x_ref[...] + _dot(mix_g, wo_ref[:GLA_WIDTH, :]) + _dot(osb_ref[...], wo_ref[GLA_WIDTH:, :])
    hn = _rmsnorm_rows(h, g2_ref[...]).astype(BF16)
    mlp = None
    for j in range(D_FF // ff_block):
        cols = slice(j * ff_block, (j + 1) * ff_block)
        u = _dot(hn, wup_ref[:, cols])
        a = jnp.square(jnp.maximum(u, 0.0)).astype(BF16)
        d = _dot(a, wdn_ref[cols, :])
        mlp = d if mlp is None else mlp + d
    y_ref[...] = h + mlp


def _finish(x2d, og, gate, osb, pw, layer, tm):
    n = x2d.shape[0]
    row = lambda w: pl.BlockSpec((tm, w), lambda i: (i, 0))
    per_layer = lambda a: pl.BlockSpec((None,) + a.shape[1:], lambda i: (layer,) + (0,) * (a.ndim - 1))
    weights = (pw['gg'], pw['wo'], pw['g2'], pw['wup'], pw['wdn'])
    return pl.pallas_call(
        functools.partial(_finish_kernel, ff_block=1024),
        grid=(n // tm,),
        in_specs=[row(D_MODEL), row(GLA_WIDTH), row(GLA_WIDTH), row(SB_WIDTH)] + [per_layer(w) for w in weights],
        out_specs=row(D_MODEL),
        out_shape=jax.ShapeDtypeStruct((n, D_MODEL), F32),
        compiler_params=pltpu.CompilerParams(dimension_semantics=("parallel",), vmem_limit_bytes=VMEM_LIMIT),
        name="finish",
    )(x2d, og, gate, osb, *weights)


def _prepare_weights(norm1_g, w_in, w_a2, b_a2, q_norm_g, k_norm_g, gla_norm_g, w_out, norm2_g, w_up, w_down):
    w_in = w_in.astype(BF16)
    row = lambda a: a.reshape(DEPTH, 1, a.shape[-1])
    return {
        'g1': row(norm1_g),
        'win': w_in,
        'wsb': w_in[:, :, SB_COL0:],
        'wa2': jnp.pad(w_a2, ((0, 0), (0, LANES - GATE_RANK), (0, 0))).astype(BF16),
        'ba2': row(b_a2),
        'qg': row(jnp.tile(q_norm_g, (1, SB_HEADS))),
        'kg': row(jnp.tile(k_norm_g, (1, SB_HEADS))),
        'gg': row(gla_norm_g),
        'wo': w_out.astype(BF16),
        'g2': row(norm2_g),
        'wup': w_up.astype(BF16),
        'wdn': w_down.astype(BF16),
    }


def _stream_layer(x, pw, layer, kv_bufs, past_k, past_v, s0, tm, tf, tg, tq, n_sub):
    bsz, t, _ = x.shape
    x2d = x.reshape(bsz * t, D_MODEL)
    gq, gk, la, gv, gate, sq, skb, svb, skf, svf = _project(x2d, bsz, pw, layer, kv_bufs, tm)
    r3 = lambda a: a.reshape(bsz, t, a.shape[-1])
    o_gla, s_new = _gla(r3(gq), r3(gk), r3(la), r3(gv), s0, layer, tg)
    o_sb = _sb(r3(sq), r3(skb), r3(svb), past_k, past_v, layer, tq, n_sub)
    y = _finish(x2d, o_gla.reshape(bsz * t, GLA_WIDTH), gate, o_sb.reshape(bsz * t, SB_WIDTH), pw, layer, tf)
    return y.reshape(bsz, t, D_MODEL), (skf, svf), s_new


def kernel(x_prompt, x_sample, cache_sb_k, cache_sb_v, state_gla, norm1_g, w_in, w_a2, b_a2, q_norm_g, k_norm_g,
           gla_norm_g, w_out, norm2_g, w_up, w_down):
    pw = _prepare_weights(norm1_g, w_in, w_a2, b_a2, q_norm_g, k_norm_g, gla_norm_g, w_out, norm2_g, w_up, w_down)
    past_k = jnp.transpose(cache_sb_k, (0, 1, 3, 4, 2))
    past_v = jnp.transpose(cache_sb_v, (0, 1, 3, 4, 2))
    xp, xs = x_prompt, x_sample
    kv_p, kv_s, states_p, states_s = None, None, [], []
    for layer in range(DEPTH):
        xp, kv_p, sp = _stream_layer(xp, pw, layer, kv_p, None, None, None, tm=512, tf=256, tg=512, tq=128, n_sub=4)
        xs, kv_s, ss = _stream_layer(xs, pw, layer, kv_s, past_k, past_v, state_gla, tm=512, tf=256, tg=CHUNK,
                                     tq=CHUNK, n_sub=1)
        states_p.append(sp)
        states_s.append(ss)
    return (xp, xs, kv_p[0], kv_p[1], jnp.stack(states_p), kv_s[0], kv_s[1], jnp.stack(states_s))
```

```python
import functools

import jax
import jax.numpy as jnp
from jax import lax
from jax.experimental import pallas as pl
from jax.experimental.pallas import tpu as pltpu

D_MODEL = 1024
DEPTH = 2
CHUNK = 64
GLA_HEADS = 4
GLA_DK = 64
GLA_DV = 128
GLA_KEY = GLA_HEADS * GLA_DK
GLA_WIDTH = GLA_HEADS * GLA_DV
GATE_RANK = 16
GATE_NORM = 16.0
SB_HEADS = 8
SB_HD = 64
SB_WIDTH = SB_HEADS * SB_HD
D_FF = 4 * D_MODEL
EPS = 1e-6

LANES = 128
ROW_TILE = 16
SB_UNIT = 3
SB_ZERO_LOG = 110.0
LOG2E = 1.4426950408889634
VMEM_LIMIT = 56 * 1024 * 1024

_GATE_COLS = 2 * GLA_KEY + 2 * GLA_WIDTH
_COL_QK = (0, 2 * GLA_KEY)
_COL_GV = (2 * GLA_KEY, 2 * GLA_KEY + GLA_WIDTH)
_COL_GATE = (2 * GLA_KEY + GLA_WIDTH, _GATE_COLS)
_COL_ALR = (_GATE_COLS, _GATE_COLS + LANES)
GLA_COLS = _COL_ALR[1]
SB_COL0 = _GATE_COLS + GATE_RANK
_COL_SQ = (0, SB_WIDTH)
_COL_SK = (SB_WIDTH, 2 * SB_WIDTH)
_COL_SV = (2 * SB_WIDTH, 3 * SB_WIDTH)

F32 = jnp.float32
BF16 = jnp.bfloat16


def _dot(a, b):
    return jnp.dot(a, b, preferred_element_type=F32)


def _dot_nt(a, b):
    return lax.dot_general(a, b, (((1,), (1,)), ((), ())), preferred_element_type=F32)


def _iota(shape, dim):
    return lax.broadcasted_iota(jnp.int32, shape, dim)


def _split3(x):
    p1 = x.astype(BF16)
    r1 = x - p1.astype(F32)
    p2 = r1.astype(BF16)
    p3 = (r1 - p2.astype(F32)).astype(BF16)
    return p1, p2, p3


def _split2(x):
    p1 = x.astype(BF16)
    p2 = (x - p1.astype(F32)).astype(BF16)
    return p1, p2


def _softplus(z):
    return jnp.maximum(z, 0.0) + jnp.log(1.0 + jnp.exp(-jnp.abs(z)))


def _rmsnorm_rows(x, g):
    ms = jnp.mean(x * x, axis=-1, keepdims=True)
    return x * lax.rsqrt(ms + EPS) * g


def _project_kernel(*refs, aliased, n_split):
    x_ref, g1_ref, wgla_ref, wsb_ref, wa2_ref, ba2_ref, qg_ref, kg_ref = refs[:8]
    gq_ref, gk_ref, la_ref, gv_ref, gate_ref, sq_ref, skb_ref, svb_ref, k4_ref, v4_ref = refs[8 + 2 * aliased:]
    rows_per_split = x_ref.shape[0] // n_split

    r = _iota((LANES, LANES), 0) // SB_HD
    c = _iota((LANES, LANES), 1) // SB_HD
    ones_bd = jnp.where(r == c, 1.0, 0.0).astype(BF16)

    def head_mean_squares(s):
        out = []
        for j in range(SB_WIDTH // LANES):
            sj = s[:, j * LANES:(j + 1) * LANES]
            hi, lo = _split2(sj * sj)
            out.append((_dot(hi, ones_bd) + _dot(lo, ones_bd)) * (1.0 / SB_HD))
        return jnp.concatenate(out, axis=-1)

    def products(xn):
        proj = lambda w_ref, cols: _dot(xn, w_ref[:, cols[0]:cols[1]])
        qk = proj(wgla_ref, _COL_QK)
        gv = proj(wgla_ref, _COL_GV)
        gate = proj(wgla_ref, _COL_GATE)
        alr = proj(wgla_ref, _COL_ALR)
        sq = proj(wsb_ref, _COL_SQ)
        sk = proj(wsb_ref, _COL_SK)
        sv = proj(wsb_ref, _COL_SV)
        y = _dot(alr.astype(BF16), wa2_ref[...]) + ba2_ref[...]
        return qk, gv, gate, sq, sk, sv, y, head_mean_squares(sq), head_mean_squares(sk)

    def store_cache_layout(ref, row0, a):
        rows = ref.reshape(x_ref.shape[0] * SB_HEADS, SB_HD)
        for h in range(SB_HEADS):
            rows[pl.ds(row0 * SB_HEADS + h, a.shape[0], stride=SB_HEADS), :] = a[:, h * SB_HD:(h + 1) * SB_HD]

    def write(row0, qk, gv, gate, sq, sk, sv, y, ms_q, ms_k):
        rows = slice(row0, row0 + rows_per_split)
        gq_ref[rows, :] = qk[:, :GLA_KEY] * (GLA_DK ** -0.5)
        gk_ref[rows, :] = qk[:, GLA_KEY:]
        gv_ref[rows, :] = gv.astype(BF16)
        gate_ref[rows, :] = gate
        svb_ref[rows, :] = sv.astype(BF16)
        store_cache_layout(v4_ref, row0, sv)
        la_ref[rows, :] = (jnp.minimum(y, 0.0) - jnp.log(1.0 + jnp.exp(-jnp.abs(y)))) * (1.0 / GATE_NORM)
        sq_ref[rows, :] = (sq * lax.rsqrt(ms_q + EPS) * qg_ref[...] * (SB_HD ** -0.5)).astype(BF16)
        sk_n = sk * lax.rsqrt(ms_k + EPS) * kg_ref[...]
        skb_ref[rows, :] = sk_n.astype(BF16)
        store_cache_layout(k4_ref, row0, sk_n)

    row0s = [i * rows_per_split for i in range(n_split)]
    xns = [_rmsnorm_rows(x_ref[r0:r0 + rows_per_split, :], g1_ref[...]).astype(BF16) for r0 in row0s]
    results = [products(xn) for xn in xns]
    for r0, res in zip(row0s, results):
        write(r0, *res)


def _project(x2d, bsz, pw, layer, kv_bufs, tm):
    n = x2d.shape[0]
    t = n // bsz
    aliased = kv_bufs is not None
    row = lambda w: pl.BlockSpec((tm, w), lambda i: (i, 0))
    per_layer = lambda a: pl.BlockSpec((None,) + a.shape[1:], lambda i: (layer,) + (0,) * (a.ndim - 1))
    weights = (pw['g1'], pw['win'], pw['wsb'], pw['wa2'], pw['ba2'], pw['qg'], pw['kg'])
    weight_specs = [per_layer(w) for w in weights]
    weight_specs[1] = pl.BlockSpec((None, D_MODEL, GLA_COLS), lambda i: (layer, 0, 0))
    out_cols = ((GLA_KEY, F32), (GLA_KEY, F32), (GLA_KEY, F32), (GLA_WIDTH, BF16), (GLA_WIDTH, F32),
                (SB_WIDTH, BF16), (SB_WIDTH, BF16), (SB_WIDTH, BF16))
    if tm <= t:
        kv_spec = pl.BlockSpec((None, 1, tm, SB_HEADS, SB_HD), lambda i: (layer, i // (t // tm), i % (t // tm), 0, 0))
    else:
        kv_spec = pl.BlockSpec((None, tm // t, t, SB_HEADS, SB_HD), lambda i: (layer, i, 0, 0, 0))
    kv_shape = jax.ShapeDtypeStruct((DEPTH, bsz, t, SB_HEADS, SB_HD), F32)
    n_in = 1 + len(weights)
    return pl.pallas_call(
        functools.partial(_project_kernel, aliased=aliased, n_split=1),
        grid=(n // tm,),
        in_specs=[row(D_MODEL)] + weight_specs + ([pl.BlockSpec(memory_space=pl.ANY)] * 2 if aliased else []),
        out_specs=[row(w) for w, _ in out_cols] + [kv_spec, kv_spec],
        out_shape=[jax.ShapeDtypeStruct((n, w), dt) for w, dt in out_cols] + [kv_shape, kv_shape],
        input_output_aliases={n_in: len(out_cols), n_in + 1: len(out_cols) + 1} if aliased else {},
        compiler_params=pltpu.CompilerParams(dimension_semantics=("parallel",), vmem_limit_bytes=VMEM_LIMIT),
        name="project",
    )(x2d, *weights, *(kv_bufs if aliased else ()))


def _gla_kernel(*refs, n_chunks, chunks_per_step, has_init):
    if has_init:
        q_ref, k_ref, la_ref, v_ref, s0_ref, o_ref, sout_ref, s_scr, b_scr = refs
    else:
        q_ref, k_ref, la_ref, v_ref, o_ref, sout_ref, s_scr, b_scr = refs
    g = pl.program_id(2)

    @pl.when(g == 0)
    def _():
        if has_init:
            s_scr[...] = s0_ref[...].reshape(2 * GLA_DK, GLA_DV)
        else:
            s_scr[...] = jnp.zeros_like(s_scr)

    rr = _iota((CHUNK, CHUNK), 0)
    cc = _iota((CHUNK, CHUNK), 1)
    tri = jnp.where(cc <= rr, 1.0, 0.0).astype(BF16)
    r = _iota((2 * LANES, 2 * LANES), 0) // GLA_DK
    c = _iota((2 * LANES, 2 * LANES), 1) // GLA_DK
    ones_bd2 = jnp.where(r == c, 1.0, 0.0).astype(BF16)
    lane = _iota((CHUNK, LANES), 1)
    lane_t = lane % CHUNK
    tile_t = _iota((ROW_TILE, LANES), 1) % CHUNK
    srow = _iota((CHUNK, LANES), 0)
    head_a = lane < GLA_DK

    n_sub = CHUNK // ROW_TILE
    sub = lambda g: slice(g * ROW_TILE, (g + 1) * ROW_TILE)

    def operands(slot, base, b):
        b2 = b * LOG2E
        b_scr[slot] = b2
        rows = pl.ds(base, CHUNK)
        q = q_ref[rows, :]
        k = k_ref[rows, :]

        b_end = jnp.concatenate(
            [jnp.broadcast_to(b[(g + 1) * ROW_TILE - 1:(g + 1) * ROW_TILE], (ROW_TILE, LANES)) for g in range(n_sub)],
            axis=0)
        k_dec = (k * jnp.exp(b_end - b)).astype(BF16)
        q_dec_rhs = []
        for g in range(n_sub - 1):
            lo = (g + 1) * ROW_TILE
            q_dec = q[lo:] * jnp.exp(b[lo:] - b[lo - 1:lo])
            pad = jnp.zeros((lo, LANES), F32)
            in_a = _iota((CHUNK - lo, LANES), 1) < GLA_DK
            q_dec_rhs.append(
                jnp.concatenate([pad, jnp.where(in_a, q_dec, 0.0), pad, jnp.where(in_a, 0.0, q_dec)],
                                axis=0).astype(BF16))

        es = []
        for t in range(CHUNK):
            g = t // ROW_TILE
            brow = b_scr[slot, t:t + 1, :]
            qrow = q_ref[pl.ds(base + t, 1), :]
            es.append(jnp.exp2(brow - b2[sub(g)]).astype(BF16) * (qrow * k[sub(g)]).astype(BF16))
        diag_lhs = jnp.concatenate(
            [jnp.concatenate([es[2 * p], es[2 * p + 1]], axis=1) for p in range(CHUNK // 2)], axis=0)

        qe = q * jnp.exp(b)
        qe_a = jnp.where(head_a, qe, 0.0).astype(BF16)
        qe_b = jnp.where(head_a, 0.0, qe).astype(BF16)
        b_t = b.T
        b_last = b_t[:, CHUNK - 1:CHUNK]
        kd_t = (k.T * jnp.exp(b_last - b_t)).astype(BF16)
        return k_dec, q_dec_rhs, diag_lhs, qe_a, qe_b, kd_t, jnp.exp(b_last)

    def scores(off_tiles, red):
        pt_tiles = list(off_tiles) + [jnp.zeros((ROW_TILE, LANES), F32)]
        for t in range(CHUNK):
            g, p, half = t // ROW_TILE, t // 2, t % 2
            r_t = red[p * ROW_TILE:(p + 1) * ROW_TILE, half * LANES:(half + 1) * LANES]
            pt_tiles[g] = jnp.where(tile_t == t, r_t, pt_tiles[g])
        pt = jnp.concatenate(pt_tiles, axis=0)
        return jnp.where(srow <= lane_t, pt, 0.0)

    def step_body(si, carry):
        slots = range(chunks_per_step)
        bases = [pl.multiple_of((si * chunks_per_step + slot) * CHUNK, CHUNK) for slot in slots]
        bs = []
        for base in bases:
            p1, p2, p3 = _split3(la_ref[pl.ds(base, CHUNK), :])
            bs.append(_dot(tri, p1) + _dot(tri, p2) + _dot(tri, p3))
        ops = [operands(slot, bases[slot], bs[slot]) for slot in slots]
        v2s = [v_ref[pl.ds(base, CHUNK), :] for base in bases]
        offs = [[_dot_nt(op[0][sub(g)], op[1][g]) for g in range(n_sub - 1)] for op in ops]
        reds = [_dot(op[2], ones_bd2) for op in ops]
        upds = [_dot(op[5], v2) for op, v2 in zip(ops, v2s)]
        pts = [scores(off, red).T.astype(BF16) for off, red in zip(offs, reds)]
        o_intras = [_dot(pt, v2) for pt, v2 in zip(pts, v2s)]

        s = s_scr[...]
        for slot in slots:
            _, _, _, qe_a, qe_b, _, decay = ops[slot]
            rows = pl.ds(bases[slot], CHUNK)
            s_bf = s.astype(BF16)
            o_ref[rows, :GLA_DV] = o_intras[slot][:CHUNK, :GLA_DV] + _dot(qe_a, s_bf)
            o_ref[rows, GLA_DV:] = o_intras[slot][CHUNK:, GLA_DV:] + _dot(qe_b, s_bf)
            upd = upds[slot]
            s = decay * s + jnp.concatenate([upd[:GLA_DK, :GLA_DV], upd[GLA_DK:, GLA_DV:]], axis=0)
        s_scr[...] = s
        return carry

    lax.fori_loop(0, n_chunks // chunks_per_step, step_body, 0)

    @pl.when(g == pl.num_programs(2) - 1)
    def _():
        sout_ref[...] = s_scr[...].reshape(2, GLA_DK, GLA_DV)


def _gla(gq, gk, la, gv, s0, layer, tg):
    bsz, t, _ = gq.shape
    has_init = s0 is not None
    n_chunks = tg // CHUNK
    chunks_per_step = 8 if n_chunks % 8 == 0 else 1
    qspec = pl.BlockSpec((None, tg, LANES), lambda b, p, g: (b, g, p))
    vspec = pl.BlockSpec((None, tg, 2 * GLA_DV), lambda b, p, g: (b, g, p))
    sspec = pl.BlockSpec((None, 2, GLA_DK, GLA_DV), lambda b, p, g: (b, p, 0, 0))
    s0spec = pl.BlockSpec((None, None, 2, GLA_DK, GLA_DV), lambda b, p, g: (layer, b, p, 0, 0))
    in_specs = [qspec, qspec, qspec, vspec] + ([s0spec] if has_init else [])
    args = (gq, gk, la, gv) + ((s0,) if has_init else ())
    return pl.pallas_call(
        functools.partial(_gla_kernel, n_chunks=n_chunks, chunks_per_step=chunks_per_step, has_init=has_init),
        grid=(bsz, GLA_HEADS // 2, t // tg),
        in_specs=in_specs,
        out_specs=[vspec, sspec],
        out_shape=[jax.ShapeDtypeStruct((bsz, t, GLA_WIDTH), F32),
                   jax.ShapeDtypeStruct((bsz, GLA_HEADS, GLA_DK, GLA_DV), F32)],
        scratch_shapes=[pltpu.VMEM((2 * GLA_DK, GLA_DV), F32), pltpu.VMEM((chunks_per_step, CHUNK, LANES), F32)],
        compiler_params=pltpu.CompilerParams(
            dimension_semantics=("parallel", "parallel", "arbitrary"), vmem_limit_bytes=VMEM_LIMIT),
        name="gla",
    )(*args)


def _cumsum_matrix(tk):
    r = _iota((tk, 2 * tk), 0)
    c = _iota((tk, 2 * tk), 1)
    return jnp.where((c >= tk) | (r > c), 1.0, 0.0).astype(BF16)


def _sb_unit(streams):
    n_keys = lambda b: b[0].shape[1] if b[4] else b[0].shape[0]
    zs = [[_dot(q2, b[0]) if b[4] else _dot_nt(q2, b[0]) for b in blocks] for q2, blocks, _, _ in streams]
    sps = []
    for z_list, (_, blocks, _, _) in zip(zs, streams):
        sp_list = []
        for z, (_, _, _, mask, _) in zip(z_list, blocks):
            sp = _softplus(z)
            sp_list.append(sp if mask is None else jnp.where(mask, sp, 0.0))
        sps.append(sp_list)
    crs = []
    for sp_list, (q2, blocks, _, _) in zip(sps, streams):
        cr_list = [None] * len(blocks)
        rows = q2.shape[0]
        for tk in sorted({n_keys(b) for b in blocks}):
            idx = [n for n, b in enumerate(blocks) if n_keys(b) == tk]
            u = blocks[idx[0]][2]
            cr = _dot(jnp.concatenate([sp_list[n] for n in idx], axis=0).astype(BF16), u)
            for m, n in enumerate(idx):
                cr_list[n] = cr[m * rows:(m + 1) * rows]
        crs.append(cr_list)
    ws = []
    carries = []
    for z_list, sp_list, cr_list, (_, blocks, carry, _) in zip(zs, sps, crs, streams):
        w_list = []
        for z, sp, cr, block in zip(z_list, sp_list, cr_list, blocks):
            tk, mask = n_keys(block), block[3]
            w = jnp.exp(z - sp - cr[:, :tk] - carry[:, :tk])
            if mask is not None:
                w = jnp.where(mask, w, 0.0)
            w_list.append(w.astype(BF16))
            rs = cr[:, tk:]
            if tk < LANES:
                rs = jnp.concatenate([rs] * (LANES // tk), axis=1)
            carry = carry + rs
        ws.append(w_list)
        carries.append(carry)
    out = []
    for w_list, carry, (_, blocks, _, acc) in zip(ws, carries, streams):
        pv = None
        for w, (_, vj, _, _, transposed) in zip(w_list, blocks):
            d = _dot_nt(w, vj) if transposed else _dot(w, vj)
            pv = d if pv is None else pv + d
        out.append((carry, acc + pv))
    return out


def _sb_kernel(*refs, tq, n_sub, n_past_blocks):
    if n_past_blocks:
        q_ref, k_ref, v_ref, pk_ref, pv_ref, o_ref = refs
    else:
        q_ref, k_ref, v_ref, o_ref = refs
    i = pl.program_id(2)
    lane = _iota((tq, LANES), 1)
    u_own = _cumsum_matrix(tq)
    strict = _iota((2 * tq, tq), 1) < _iota((2 * tq, tq), 0) % tq
    zeros = jnp.zeros((2 * tq, LANES), F32)

    def stacked_heads(r):
        q = q_ref[r * tq:(r + 1) * tq, :]
        zero_q = jnp.zeros_like(q)
        return jnp.concatenate([jnp.where(lane < SB_HD, q, zero_q), jnp.where(lane < SB_HD, zero_q, q)], axis=0)

    q2s = [stacked_heads(r) for r in range(n_sub)]

    if n_past_blocks:
        u_past = _cumsum_matrix(LANES)

        def past_block(ref, j):
            return ref[:, :, j * LANES:(j + 1) * LANES].reshape(2 * SB_HD, LANES).astype(BF16)

        blocks = [(k_ref[...], v_ref[...], u_own, strict, False)]
        for j in reversed(range(n_past_blocks)):
            blocks.append((past_block(pk_ref, j), past_block(pv_ref, j), u_past, None, True))
        accs = [_sb_unit([(q2s[0], blocks, zeros, zeros)])[0][1]]
    else:
        def load_unit(r, n, first):
            blocks = []
            for kk in range(SB_UNIT):
                j = i * n_sub + r - n * SB_UNIT - kk
                rows = pl.ds(pl.multiple_of(jnp.maximum(j, 0) * tq, tq), tq)
                vj = v_ref[rows, :]
                diag = first and kk == 0
                if not diag:
                    vj = jnp.where(j >= 0, vj, jnp.zeros_like(vj))
                blocks.append((k_ref[rows, :], vj, u_own, strict if diag else None, False))
            return blocks

        def min_carry(state):
            m = state[0][0]
            for carry, _ in state[1:]:
                m = jnp.minimum(m, carry)
            return jnp.min(m)

        state = _sb_unit([(q2s[r], load_unit(r, 0, True), zeros, zeros) for r in range(n_sub)])
        n_units = (i * n_sub + n_sub - 1 + SB_UNIT) // SB_UNIT

        def cond(st):
            return (st[0] < n_units) & (st[1] < SB_ZERO_LOG)

        def body(st):
            n, _, state = st
            state = _sb_unit([(q2s[r], load_unit(r, n, False), *state[r]) for r in range(n_sub)])
            return n + 1, min_carry(state), state

        state = lax.while_loop(cond, body, (jnp.int32(1), min_carry(state), state))[2]
        accs = [acc for _, acc in state]
    for r, acc in enumerate(accs):
        o_ref[r * tq:(r + 1) * tq, :] = jnp.where(lane < SB_HD, acc[:tq], acc[tq:]).astype(o_ref.dtype)


def _sb(q, k, v, past_k, past_v, layer, tq, n_sub):
    bsz, t, _ = q.shape
    n_past_blocks = 0 if past_k is None else past_k.shape[-1] // LANES
    qspec = pl.BlockSpec((None, tq * n_sub, LANES), lambda b, p, i: (b, i, p))
    kspec = pl.BlockSpec((None, t, LANES), lambda b, p, i: (b, 0, p))
    in_specs = [qspec, kspec, kspec]
    args = (q, k, v)
    if n_past_blocks:
        pspec = pl.BlockSpec((None, None, 2, SB_HD, past_k.shape[-1]), lambda b, p, i: (layer, b, p, 0, 0))
        in_specs += [pspec, pspec]
        args += (past_k, past_v)
    return pl.pallas_call(
        functools.partial(_sb_kernel, tq=tq, n_sub=n_sub, n_past_blocks=n_past_blocks),
        grid=(bsz, SB_WIDTH // LANES, t // (tq * n_sub)),
        in_specs=in_specs,
        out_specs=qspec,
        out_shape=jax.ShapeDtypeStruct((bsz, t, SB_WIDTH), BF16),
        compiler_params=pltpu.CompilerParams(
            dimension_semantics=("parallel", "parallel", "arbitrary"), vmem_limit_bytes=VMEM_LIMIT),
        name="sb",
    )(*args)


def _finish_kernel(x_ref, og_ref, gate_ref, osb_ref, gg_ref, wo_ref, g2_ref, wup_ref, wdn_ref, y_ref, *, ff_block):
    og = og_ref[...]
    normed = [_rmsnorm_rows(og[:, h * GLA_DV:(h + 1) * GLA_DV], gg_ref[...]) for h in range(GLA_HEADS)]
    gate = gate_ref[...]
    mix_g = (jnp.concatenate(normed, axis=-1) * (gate * jax.nn.sigmoid(gate))).astype(BF16)
    h = x_ref[...] + _dot(mix_g, wo_ref[:GLA_WIDTH, :]) + _dot(osb_ref[...], wo_ref[GLA_WIDTH:, :])
    hn = _rmsnorm_rows(h, g2_ref[...]).astype(BF16)
    mlp = None
    for j in range(D_FF // ff_block):
        cols = slice(j * ff_block, (j + 1) * ff_block)
        u = _dot(hn, wup_ref[:, cols])
        a = jnp.square(jnp.maximum(u, 0.0)).astype(BF16)
        d = _dot(a, wdn_ref[cols, :])
        mlp = d if mlp is None else mlp + d
    y_ref[...] = h + mlp


def _finish(x2d, og, gate, osb, pw, layer, tm):
    n = x2d.shape[0]
    row = lambda w: pl.BlockSpec((tm, w), lambda i: (i, 0))
    per_layer = lambda a: pl.BlockSpec((None,) + a.shape[1:], lambda i: (layer,) + (0,) * (a.ndim - 1),
                                       pipeline_mode=pl.Buffered(1))
    weights = (pw['gg'], pw['wo'], pw['g2'], pw['wup'], pw['wdn'])
    return pl.pallas_call(
        functools.partial(_finish_kernel, ff_block=1024),
        grid=(n // tm,),
        in_specs=[row(D_MODEL), row(GLA_WIDTH), row(GLA_WIDTH), row(SB_WIDTH)] + [per_layer(w) for w in weights],
        out_specs=row(D_MODEL),
        out_shape=jax.ShapeDtypeStruct((n, D_MODEL), F32),
        compiler_params=pltpu.CompilerParams(dimension_semantics=("parallel",), vmem_limit_bytes=VMEM_LIMIT),
        name="finish",
    )(x2d, og, gate, osb, *weights)


def _prepare_weights(norm1_g, w_in, w_a2, b_a2, q_norm_g, k_norm_g, gla_norm_g, w_out, norm2_g, w_up, w_down):
    w_in = w_in.astype(BF16)
    row = lambda a: a.reshape(DEPTH, 1, a.shape[-1])
    return {
        'g1': row(norm1_g),
        'win': w_in,
        'wsb': w_in[:, :, SB_COL0:],
        'wa2': jnp.pad(w_a2, ((0, 0), (0, LANES - GATE_RANK), (0, 0))).astype(BF16),
        'ba2': row(b_a2),
        'qg': row(jnp.tile(q_norm_g, (1, SB_HEADS))),
        'kg': row(jnp.tile(k_norm_g, (1, SB_HEADS))),
        'gg': row(gla_norm_g),
        'wo': w_out.astype(BF16),
        'g2': row(norm2_g),
        'wup': w_up.astype(BF16),
        'wdn': w_down.astype(BF16),
    }


def _stream_layer(x, pw, layer, kv_bufs, past_k, past_v, s0, tm, tf, tg, tq, n_sub):
    bsz, t, _ = x.shape
    x2d = x.reshape(bsz * t, D_MODEL)
    gq, gk, la, gv, gate, sq, skb, svb, skf, svf = _project(x2d, bsz, pw, layer, kv_bufs, tm)
    r3 = lambda a: a.reshape(bsz, t, a.shape[-1])
    o_gla, s_new = _gla(r3(gq), r3(gk), r3(la), r3(gv), s0, layer, tg)
    o_sb = _sb(r3(sq), r3(skb), r3(svb), past_k, past_v, layer, tq, n_sub)
    y = _finish(x2d, o_gla.reshape(bsz * t, GLA_WIDTH), gate, o_sb.reshape(bsz * t, SB_WIDTH), pw, layer, tf)
    return y.reshape(bsz, t, D_MODEL), (skf, svf), s_new


def kernel(x_prompt, x_sample, cache_sb_k, cache_sb_v, state_gla, norm1_g, w_in, w_a2, b_a2, q_norm_g, k_norm_g,
           gla_norm_g, w_out, norm2_g, w_up, w_down):
    pw = _prepare_weights(norm1_g, w_in, w_a2, b_a2, q_norm_g, k_norm_g, gla_norm_g, w_out, norm2_g, w_up, w_down)
    past_k = jnp.transpose(cache_sb_k, (0, 1, 3, 4, 2))
    past_v = jnp.transpose(cache_sb_v, (0, 1, 3, 4, 2))
    xp, xs = x_prompt, x_sample
    kv_p, kv_s, states_p, states_s = None, None, [], []
    for layer in range(DEPTH):
        xp, kv_p, sp = _stream_layer(xp, pw, layer, kv_p, None, None, None, tm=512, tf=512, tg=512, tq=128, n_sub=4)
        xs, kv_s, ss = _stream_layer(xs, pw, layer, kv_s, past_k, past_v, state_gla, tm=512, tf=256, tg=CHUNK,
                                     tq=CHUNK, n_sub=1)
        states_p.append(sp)
        states_s.append(ss)
    return (xp, xs, kv_p[0], kv_p[1], jnp.stack(states_p), kv_s[0], kv_s[1], jnp.stack(states_s))
```

```python
import functools

import jax
import jax.numpy as jnp
from jax import lax
from jax.experimental import pallas as pl
from jax.experimental.pallas import tpu as pltpu

D_MODEL = 1024
DEPTH = 2
CHUNK = 64
GLA_HEADS = 4
GLA_DK = 64
GLA_DV = 128
GLA_KEY = GLA_HEADS * GLA_DK
GLA_WIDTH = GLA_HEADS * GLA_DV
GATE_RANK = 16
GATE_NORM = 16.0
SB_HEADS = 8
SB_HD = 64
SB_WIDTH = SB_HEADS * SB_HD
D_FF = 4 * D_MODEL
EPS = 1e-6

LANES = 128
ROW_TILE = 16
SB_UNIT = 3
SB_ZERO_LOG = 110.0
LOG2E = 1.4426950408889634
VMEM_LIMIT = 56 * 1024 * 1024

_GATE_COLS = 2 * GLA_KEY + 2 * GLA_WIDTH
_COL_QK = (0, 2 * GLA_KEY)
_COL_GV = (2 * GLA_KEY, 2 * GLA_KEY + GLA_WIDTH)
_COL_GATE = (2 * GLA_KEY + GLA_WIDTH, _GATE_COLS)
_COL_ALR = (_GATE_COLS, _GATE_COLS + LANES)
GLA_COLS = _COL_ALR[1]
SB_COL0 = _GATE_COLS + GATE_RANK
_COL_SQ = (0, SB_WIDTH)
_COL_SK = (SB_WIDTH, 2 * SB_WIDTH)
_COL_SV = (2 * SB_WIDTH, 3 * SB_WIDTH)

F32 = jnp.float32
BF16 = jnp.bfloat16


def _dot(a, b):
    return jnp.dot(a, b, preferred_element_type=F32)


def _dot_nt(a, b):
    return lax.dot_general(a, b, (((1,), (1,)), ((), ())), preferred_element_type=F32)


def _iota(shape, dim):
    return lax.broadcasted_iota(jnp.int32, shape, dim)


def _split3(x):
    p1 = x.astype(BF16)
    r1 = x - p1.astype(F32)
    p2 = r1.astype(BF16)
    p3 = (r1 - p2.astype(F32)).astype(BF16)
    return p1, p2, p3


def _split2(x):
    p1 = x.astype(BF16)
    p2 = (x - p1.astype(F32)).astype(BF16)
    return p1, p2


def _softplus(z):
    return jnp.maximum(z, 0.0) + jnp.log(1.0 + jnp.exp(-jnp.abs(z)))


def _rmsnorm_rows(x, g):
    ms = jnp.mean(x * x, axis=-1, keepdims=True)
    return x * lax.rsqrt(ms + EPS) * g


def _project_kernel(*refs, aliased, n_split):
    x_ref, g1_ref, wgla_ref, wsb_ref, wa2_ref, ba2_ref, qg_ref, kg_ref = refs[:8]
    gq_ref, gk_ref, la_ref, gv_ref, gate_ref, sq_ref, skb_ref, svb_ref, k4_ref, v4_ref = refs[8 + 2 * aliased:]
    rows_per_split = x_ref.shape[0] // n_split

    r = _iota((LANES, LANES), 0) // SB_HD
    c = _iota((LANES, LANES), 1) // SB_HD
    ones_bd = jnp.where(r == c, 1.0, 0.0).astype(BF16)

    def head_mean_squares(s):
        out = []
        for j in range(SB_WIDTH // LANES):
            sj = s[:, j * LANES:(j + 1) * LANES]
            hi, lo = _split2(sj * sj)
            out.append((_dot(hi, ones_bd) + _dot(lo, ones_bd)) * (1.0 / SB_HD))
        return jnp.concatenate(out, axis=-1)

    def products(xn):
        proj = lambda w_ref, cols: _dot(xn, w_ref[:, cols[0]:cols[1]])
        qk = proj(wgla_ref, _COL_QK)
        gv = proj(wgla_ref, _COL_GV)
        gate = proj(wgla_ref, _COL_GATE)
        alr = proj(wgla_ref, _COL_ALR)
        sq = proj(wsb_ref, _COL_SQ)
        sk = proj(wsb_ref, _COL_SK)
        sv = proj(wsb_ref, _COL_SV)
        y = _dot(alr.astype(BF16), wa2_ref[...]) + ba2_ref[...]
        return qk, gv, gate, sq, sk, sv, y, head_mean_squares(sq), head_mean_squares(sk)

    def store_cache_layout(ref, row0, a):
        rows = ref.reshape(x_ref.shape[0] * SB_HEADS, SB_HD)
        for h in range(SB_HEADS):
            rows[pl.ds(row0 * SB_HEADS + h, a.shape[0], stride=SB_HEADS), :] = a[:, h * SB_HD:(h + 1) * SB_HD]

    def write(row0, qk, gv, gate, sq, sk, sv, y, ms_q, ms_k):
        rows = slice(row0, row0 + rows_per_split)
        gq_ref[rows, :] = qk[:, :GLA_KEY] * (GLA_DK ** -0.5)
        gk_ref[rows, :] = qk[:, GLA_KEY:]
        gv_ref[rows, :] = gv.astype(BF16)
        gate_ref[rows, :] = gate
        svb_ref[rows, :] = sv.astype(BF16)
        store_cache_layout(v4_ref, row0, sv)
        la_ref[rows, :] = (jnp.minimum(y, 0.0) - jnp.log(1.0 + jnp.exp(-jnp.abs(y)))) * (1.0 / GATE_NORM)
        sq_ref[rows, :] = (sq * lax.rsqrt(ms_q + EPS) * qg_ref[...] * (SB_HD ** -0.5)).astype(BF16)
        sk_n = sk * lax.rsqrt(ms_k + EPS) * kg_ref[...]
        skb_ref[rows, :] = sk_n.astype(BF16)
        store_cache_layout(k4_ref, row0, sk_n)

    row0s = [i * rows_per_split for i in range(n_split)]
    xns = [_rmsnorm_rows(x_ref[r0:r0 + rows_per_split, :], g1_ref[...]).astype(BF16) for r0 in row0s]
    results = [products(xn) for xn in xns]
    for r0, res in zip(row0s, results):
        write(r0, *res)


def _project(x2d, bsz, pw, layer, kv_bufs, tm):
    n = x2d.shape[0]
    t = n // bsz
    aliased = kv_bufs is not None
    row = lambda w: pl.BlockSpec((tm, w), lambda i: (i, 0))
    per_layer = lambda a: pl.BlockSpec((None,) + a.shape[1:], lambda i: (layer,) + (0,) * (a.ndim - 1))
    weights = (pw['g1'], pw['win'], pw['wsb'], pw['wa2'], pw['ba2'], pw['qg'], pw['kg'])
    weight_specs = [per_layer(w) for w in weights]
    weight_specs[1] = pl.BlockSpec((None, D_MODEL, GLA_COLS), lambda i: (layer, 0, 0))
    out_cols = ((GLA_KEY, F32), (GLA_KEY, F32), (GLA_KEY, F32), (GLA_WIDTH, BF16), (GLA_WIDTH, F32),
                (SB_WIDTH, BF16), (SB_WIDTH, BF16), (SB_WIDTH, BF16))
    if tm <= t:
        kv_spec = pl.BlockSpec((None, 1, tm, SB_HEADS, SB_HD), lambda i: (layer, i // (t // tm), i % (t // tm), 0, 0))
    else:
        kv_spec = pl.BlockSpec((None, tm // t, t, SB_HEADS, SB_HD), lambda i: (layer, i, 0, 0, 0))
    kv_shape = jax.ShapeDtypeStruct((DEPTH, bsz, t, SB_HEADS, SB_HD), F32)
    n_in = 1 + len(weights)
    return pl.pallas_call(
        functools.partial(_project_kernel, aliased=aliased, n_split=1),
        grid=(n // tm,),
        in_specs=[row(D_MODEL)] + weight_specs + ([pl.BlockSpec(memory_space=pl.ANY)] * 2 if aliased else []),
        out_specs=[row(w) for w, _ in out_cols] + [kv_spec, kv_spec],
        out_shape=[jax.ShapeDtypeStruct((n, w), dt) for w, dt in out_cols] + [kv_shape, kv_shape],
        input_output_aliases={n_in: len(out_cols), n_in + 1: len(out_cols) + 1} if aliased else {},
        compiler_params=pltpu.CompilerParams(dimension_semantics=("parallel",), vmem_limit_bytes=VMEM_LIMIT),
        name="project",
    )(x2d, *weights, *(kv_bufs if aliased else ()))


def _gla_kernel(*refs, n_chunks, chunks_per_step, has_init):
    if has_init:
        q_ref, k_ref, la_ref, v_ref, s0_ref, o_ref, sout_ref, s_scr, b_scr = refs
    else:
        q_ref, k_ref, la_ref, v_ref, o_ref, sout_ref, s_scr, b_scr = refs
    g = pl.program_id(2)

    @pl.when(g == 0)
    def _():
        if has_init:
            s_scr[...] = s0_ref[...].reshape(2 * GLA_DK, GLA_DV)
        else:
            s_scr[...] = jnp.zeros_like(s_scr)

    rr = _iota((CHUNK, CHUNK), 0)
    cc = _iota((CHUNK, CHUNK), 1)
    tri = jnp.where(cc <= rr, 1.0, 0.0).astype(BF16)
    r = _iota((2 * LANES, 2 * LANES), 0) // GLA_DK
    c = _iota((2 * LANES, 2 * LANES), 1) // GLA_DK
    ones_bd2 = jnp.where(r == c, 1.0, 0.0).astype(BF16)
    lane = _iota((CHUNK, LANES), 1)
    lane_t = lane % CHUNK
    tile_t = _iota((ROW_TILE, LANES), 1) % CHUNK
    srow = _iota((CHUNK, LANES), 0)
    head_a = lane < GLA_DK

    n_sub = CHUNK // ROW_TILE
    sub = lambda g: slice(g * ROW_TILE, (g + 1) * ROW_TILE)

    def operands(slot, base, b):
        b2 = b * LOG2E
        b_scr[slot] = b2
        rows = pl.ds(base, CHUNK)
        q = q_ref[rows, :]
        k = k_ref[rows, :]

        b_end = jnp.concatenate(
            [jnp.broadcast_to(b[(g + 1) * ROW_TILE - 1:(g + 1) * ROW_TILE], (ROW_TILE, LANES)) for g in range(n_sub)],
            axis=0)
        k_dec = (k * jnp.exp(b_end - b)).astype(BF16)
        q_dec_rhs = []
        for g in range(n_sub - 1):
            lo = (g + 1) * ROW_TILE
            q_dec = q[lo:] * jnp.exp(b[lo:] - b[lo - 1:lo])
            pad = jnp.zeros((lo, LANES), F32)
            in_a = _iota((CHUNK - lo, LANES), 1) < GLA_DK
            q_dec_rhs.append(
                jnp.concatenate([pad, jnp.where(in_a, q_dec, 0.0), pad, jnp.where(in_a, 0.0, q_dec)],
                                axis=0).astype(BF16))

        es = []
        for t in range(CHUNK):
            g = t // ROW_TILE
            brow = b_scr[slot, t:t + 1, :]
            qrow = q_ref[pl.ds(base + t, 1), :]
            es.append(jnp.exp2(brow - b2[sub(g)]).astype(BF16) * (qrow * k[sub(g)]).astype(BF16))
        diag_lhs = jnp.concatenate(
            [jnp.concatenate([es[2 * p], es[2 * p + 1]], axis=1) for p in range(CHUNK // 2)], axis=0)

        qe = q * jnp.exp(b)
        qe_a = jnp.where(head_a, qe, 0.0).astype(BF16)
        qe_b = jnp.where(head_a, 0.0, qe).astype(BF16)
        b_t = b.T
        b_last = b_t[:, CHUNK - 1:CHUNK]
        kd_t = (k.T * jnp.exp(b_last - b_t)).astype(BF16)
        return k_dec, q_dec_rhs, diag_lhs, qe_a, qe_b, kd_t, jnp.exp(b_last)

    def scores(off_tiles, red):
        pt_tiles = list(off_tiles) + [jnp.zeros((ROW_TILE, LANES), F32)]
        for t in range(CHUNK):
            g, p, half = t // ROW_TILE, t // 2, t % 2
            r_t = red[p * ROW_TILE:(p + 1) * ROW_TILE, half * LANES:(half + 1) * LANES]
            pt_tiles[g] = jnp.where(tile_t == t, r_t, pt_tiles[g])
        pt = jnp.concatenate(pt_tiles, axis=0)
        return jnp.where(srow <= lane_t, pt, 0.0)

    def step_body(si, carry):
        slots = range(chunks_per_step)
        bases = [pl.multiple_of((si * chunks_per_step + slot) * CHUNK, CHUNK) for slot in slots]
        bs = []
        for base in bases:
            p1, p2, p3 = _split3(la_ref[pl.ds(base, CHUNK), :])
            bs.append(_dot(tri, p1) + _dot(tri, p2) + _dot(tri, p3))
        ops = [operands(slot, bases[slot], bs[slot]) for slot in slots]
        v2s = [v_ref[pl.ds(base, CHUNK), :] for base in bases]
        offs = [[_dot_nt(op[0][sub(g)], op[1][g]) for g in range(n_sub - 1)] for op in ops]
        reds = [_dot(op[2], ones_bd2) for op in ops]
        upds = [_dot(op[5], v2) for op, v2 in zip(ops, v2s)]
        pts = [scores(off, red).T.astype(BF16) for off, red in zip(offs, reds)]
        o_intras = [_dot(pt, v2) for pt, v2 in zip(pts, v2s)]

        s = s_scr[...]
        for slot in slots:
            _, _, _, qe_a, qe_b, _, decay = ops[slot]
            rows = pl.ds(bases[slot], CHUNK)
            s_bf = s.astype(BF16)
            o_ref[rows, :GLA_DV] = o_intras[slot][:CHUNK, :GLA_DV] + _dot(qe_a, s_bf)
            o_ref[rows, GLA_DV:] = o_intras[slot][CHUNK:, GLA_DV:] + _dot(qe_b, s_bf)
            upd = upds[slot]
            s = decay * s + jnp.concatenate([upd[:GLA_DK, :GLA_DV], upd[GLA_DK:, GLA_DV:]], axis=0)
        s_scr[...] = s
        return carry

    lax.fori_loop(0, n_chunks // chunks_per_step, step_body, 0)

    @pl.when(g == pl.num_programs(2) - 1)
    def _():
        sout_ref[...] = s_scr[...].reshape(2, GLA_DK, GLA_DV)


def _gla(gq, gk, la, gv, s0, layer, tg):
    bsz, t, _ = gq.shape
    has_init = s0 is not None
    n_chunks = tg // CHUNK
    chunks_per_step = 16 if n_chunks % 16 == 0 else 1
    qspec = pl.BlockSpec((None, tg, LANES), lambda b, p, g: (b, g, p))
    vspec = pl.BlockSpec((None, tg, 2 * GLA_DV), lambda b, p, g: (b, g, p))
    sspec = pl.BlockSpec((None, 2, GLA_DK, GLA_DV), lambda b, p, g: (b, p, 0, 0))
    s0spec = pl.BlockSpec((None, None, 2, GLA_DK, GLA_DV), lambda b, p, g: (layer, b, p, 0, 0))
    in_specs = [qspec, qspec, qspec, vspec] + ([s0spec] if has_init else [])
    args = (gq, gk, la, gv) + ((s0,) if has_init else ())
    return pl.pallas_call(
        functools.partial(_gla_kernel, n_chunks=n_chunks, chunks_per_step=chunks_per_step, has_init=has_init),
        grid=(bsz, GLA_HEADS // 2, t // tg),
        in_specs=in_specs,
        out_specs=[vspec, sspec],
        out_shape=[jax.ShapeDtypeStruct((bsz, t, GLA_WIDTH), F32),
                   jax.ShapeDtypeStruct((bsz, GLA_HEADS, GLA_DK, GLA_DV), F32)],
        scratch_shapes=[pltpu.VMEM((2 * GLA_DK, GLA_DV), F32), pltpu.VMEM((chunks_per_step, CHUNK, LANES), F32)],
        compiler_params=pltpu.CompilerParams(
            dimension_semantics=("parallel", "parallel", "arbitrary"), vmem_limit_bytes=VMEM_LIMIT),
        name="gla",
    )(*args)


def _cumsum_matrix(tk):
    r = _iota((tk, 2 * tk), 0)
    c = _iota((tk, 2 * tk), 1)
    return jnp.where((c >= tk) | (r > c), 1.0, 0.0).astype(BF16)


def _sb_unit(streams):
    n_keys = lambda b: b[0].shape[1] if b[4] else b[0].shape[0]

    def taking_part(a, block):
        tq, hr = a.shape[0] // 2, block[5]
        return a if hr is None else jnp.concatenate([a[:hr], a[tq:tq + hr]], axis=0)

    def all_rows(a, block, tq):
        hr = block[5]
        if hr is None:
            return a
        pad = jnp.zeros((tq - hr, a.shape[1]), a.dtype)
        return jnp.concatenate([a[:hr], pad, a[hr:], pad], axis=0)

    zs = [[_dot(taking_part(q2, b), b[0]) if b[4] else _dot_nt(taking_part(q2, b), b[0]) for b in blocks]
          for q2, blocks, _, _ in streams]
    sps = []
    for z_list, (_, blocks, _, _) in zip(zs, streams):
        sp_list = []
        for z, block in zip(z_list, blocks):
            sp = _softplus(z)
            sp_list.append(sp if block[3] is None else jnp.where(block[3], sp, 0.0))
        sps.append(sp_list)
    crs = []
    for sp_list, (_, blocks, _, _) in zip(sps, streams):
        cr_list = [None] * len(blocks)
        for tk in sorted({n_keys(b) for b in blocks}):
            idx = [n for n, b in enumerate(blocks) if n_keys(b) == tk]
            u = blocks[idx[0]][2]
            cr = _dot(jnp.concatenate([sp_list[n] for n in idx], axis=0).astype(BF16), u)
            row0 = 0
            for n in idx:
                cr_list[n] = cr[row0:row0 + sp_list[n].shape[0]]
                row0 += sp_list[n].shape[0]
        crs.append(cr_list)
    ws = []
    carries = []
    for z_list, sp_list, cr_list, (q2, blocks, carry, _) in zip(zs, sps, crs, streams):
        w_list = []
        for z, sp, cr, block in zip(z_list, sp_list, cr_list, blocks):
            tk, mask = n_keys(block), block[3]
            w = jnp.exp(z - sp - cr[:, :tk] - taking_part(carry, block)[:, :tk])
            if mask is not None:
                w = jnp.where(mask, w, 0.0)
            w_list.append(w.astype(BF16))
            rs = cr[:, tk:]
            if tk < LANES:
                rs = jnp.concatenate([rs] * (LANES // tk), axis=1)
            carry = carry + all_rows(rs, block, q2.shape[0] // 2)
        ws.append(w_list)
        carries.append(carry)
    out = []
    for w_list, carry, (q2, blocks, _, acc) in zip(ws, carries, streams):
        pv = None
        for w, block in zip(w_list, blocks):
            d = all_rows(_dot_nt(w, block[1]) if block[4] else _dot(w, block[1]), block, q2.shape[0] // 2)
            pv = d if pv is None else pv + d
        out.append((carry, acc + pv))
    return out


def _sb_kernel(*refs, tq, n_sub, n_past_blocks):
    if n_past_blocks:
        q_ref, k_ref, v_ref, pk_ref, pv_ref, o_ref = refs
    else:
        q_ref, k_ref, v_ref, o_ref = refs
    i = pl.program_id(2)
    lane = _iota((tq, LANES), 1)
    u_own = _cumsum_matrix(tq)
    strict = _iota((2 * tq, tq), 1) < _iota((2 * tq, tq), 0) % tq
    zeros = jnp.zeros((2 * tq, LANES), F32)

    def stacked_heads(r):
        q = q_ref[r * tq:(r + 1) * tq, :]
        zero_q = jnp.zeros_like(q)
        return jnp.concatenate([jnp.where(lane < SB_HD, q, zero_q), jnp.where(lane < SB_HD, zero_q, q)], axis=0)

    q2s = [stacked_heads(r) for r in range(n_sub)]

    if n_past_blocks:
        u_past = _cumsum_matrix(LANES)

        def past_block(ref, j):
            return ref[:, :, j * LANES:(j + 1) * LANES].reshape(2 * SB_HD, LANES).astype(BF16)

        blocks = [(k_ref[...], v_ref[...], u_own, strict, False, None)]
        for j in reversed(range(n_past_blocks)):
            blocks.append((past_block(pk_ref, j), past_block(pv_ref, j), u_past, None, True, None))
        accs = [_sb_unit([(q2s[0], blocks, zeros, zeros)])[0][1]]
    else:
        half = tq // 2
        row_in_block = _iota((2 * tq, tq), 0) % tq

        def key_block(j, mask, head_rows):
            rows = pl.ds(pl.multiple_of(jnp.maximum(j, 0) * tq, tq), tq)
            vj = jnp.where(j >= 0, v_ref[rows, :], jnp.zeros((tq, LANES), BF16))
            return (k_ref[rows, :], vj, u_own, mask, False, head_rows)

        def first_unit(r):
            qi = i * n_sub + r
            return [key_block(qi, strict, None), key_block(qi - 1, None, None), key_block(qi - 2, None, half)]

        def later_unit(r, n):
            j0 = i * n_sub + r - 2 - (n - 1) * SB_UNIT
            blocks = [key_block(j0, row_in_block >= jnp.where(n > 1, 0, half), None)]
            return blocks + [key_block(j0 - kk, None, None) for kk in range(1, SB_UNIT)]

        def min_carry(state):
            m = state[0][0]
            for carry, _ in state[1:]:
                m = jnp.minimum(m, carry)
            return jnp.min(m)

        state = _sb_unit([(q2s[r], first_unit(r), zeros, zeros) for r in range(n_sub)])
        blocks_left = jnp.maximum(i * n_sub + n_sub - 2, 0)
        n_units = 1 + (blocks_left + SB_UNIT - 1) // SB_UNIT

        def cond(st):
            return (st[0] < n_units) & (st[1] < SB_ZERO_LOG)

        def body(st):
            n, _, state = st
            state = _sb_unit([(q2s[r], later_unit(r, n), *state[r]) for r in range(n_sub)])
            return n + 1, min_carry(state), state

        state = lax.while_loop(cond, body, (jnp.int32(1), min_carry(state), state))[2]
        accs = [acc for _, acc in state]
    for r, acc in enumerate(accs):
        o_ref[r * tq:(r + 1) * tq, :] = jnp.where(lane < SB_HD, acc[:tq], acc[tq:]).astype(o_ref.dtype)


def _sb(q, k, v, past_k, past_v, layer, tq, n_sub):
    bsz, t, _ = q.shape
    n_past_blocks = 0 if past_k is None else past_k.shape[-1] // LANES
    qspec = pl.BlockSpec((None, tq * n_sub, LANES), lambda b, p, i: (b, i, p))
    kspec = pl.BlockSpec((None, t, LANES), lambda b, p, i: (b, 0, p))
    in_specs = [qspec, kspec, kspec]
    args = (q, k, v)
    if n_past_blocks:
        pspec = pl.BlockSpec((None, None, 2, SB_HD, past_k.shape[-1]), lambda b, p, i: (layer, b, p, 0, 0))
        in_specs += [pspec, pspec]
        args += (past_k, past_v)
    return pl.pallas_call(
        functools.partial(_sb_kernel, tq=tq, n_sub=n_sub, n_past_blocks=n_past_blocks),
        grid=(bsz, SB_WIDTH // LANES, t // (tq * n_sub)),
        in_specs=in_specs,
        out_specs=qspec,
        out_shape=jax.ShapeDtypeStruct((bsz, t, SB_WIDTH), BF16),
        compiler_params=pltpu.CompilerParams(
            dimension_semantics=("parallel", "parallel", "arbitrary"), vmem_limit_bytes=VMEM_LIMIT),
        name="sb",
    )(*args)


def _finish_kernel(x_ref, og_ref, gate_ref, osb_ref, gg_ref, wo_ref, g2_ref, wup_ref, wdn_ref, y_ref, *, ff_block):
    og = og_ref[...]
    normed = [_rmsnorm_rows(og[:, h * GLA_DV:(h + 1) * GLA_DV], gg_ref[...]) for h in range(GLA_HEADS)]
    gate = gate_ref[...]
    mix_g = (jnp.concatenate(normed, axis=-1) * (gate * jax.nn.sigmoid(gate))).astype(BF16)
    h = x_ref[...] + _dot(mix_g, wo_ref[:GLA_WIDTH, :]) + _dot(osb_ref[...], wo_ref[GLA_WIDTH:, :])
    hn = _rmsnorm_rows(h, g2_ref[...]).astype(BF16)
    mlp = None
    for j in range(D_FF // ff_block):
        cols = slice(j * ff_block, (j + 1) * ff_block)
        u = _dot(hn, wup_ref[:, cols])
        a = jnp.square(jnp.maximum(u, 0.0)).astype(BF16)
        d = _dot(a, wdn_ref[cols, :])
        mlp = d if mlp is None else mlp + d
    y_ref[...] = h + mlp


def _finish(x2d, og, gate, osb, pw, layer, tm):
    n = x2d.shape[0]
    row = lambda w: pl.BlockSpec((tm, w), lambda i: (i, 0))
    per_layer = lambda a: pl.BlockSpec((None,) + a.shape[1:], lambda i: (layer,) + (0,) * (a.ndim - 1),
                                       pipeline_mode=pl.Buffered(1))
    weights = (pw['gg'], pw['wo'], pw['g2'], pw['wup'], pw['wdn'])
    return pl.pallas_call(
        functools.partial(_finish_kernel, ff_block=1024),
        grid=(n // tm,),
        in_specs=[row(D_MODEL), row(GLA_WIDTH), row(GLA_WIDTH), row(SB_WIDTH)] + [per_layer(w) for w in weights],
        out_specs=row(D_MODEL),
        out_shape=jax.ShapeDtypeStruct((n, D_MODEL), F32),
        compiler_params=pltpu.CompilerParams(dimension_semantics=("parallel",), vmem_limit_bytes=VMEM_LIMIT),
        name="finish",
    )(x2d, og, gate, osb, *weights)


def _prepare_weights(norm1_g, w_in, w_a2, b_a2, q_norm_g, k_norm_g, gla_norm_g, w_out, norm2_g, w_up, w_down):
    w_in = w_in.astype(BF16)
    row = lambda a: a.reshape(DEPTH, 1, a.shape[-1])
    return {
        'g1': row(norm1_g),
        'win': w_in,
        'wsb': w_in[:, :, SB_COL0:],
        'wa2': jnp.pad(w_a2, ((0, 0), (0, LANES - GATE_RANK), (0, 0))).astype(BF16),
        'ba2': row(b_a2),
        'qg': row(jnp.tile(q_norm_g, (1, SB_HEADS))),
        'kg': row(jnp.tile(k_norm_g, (1, SB_HEADS))),
        'gg': row(gla_norm_g),
        'wo': w_out.astype(BF16),
        'g2': row(norm2_g),
        'wup': w_up.astype(BF16),
        'wdn': w_down.astype(BF16),
    }


def _stream_layer(x, pw, layer, kv_bufs, past_k, past_v, s0, tm, tf, tg, tq, n_sub):
    bsz, t, _ = x.shape
    x2d = x.reshape(bsz * t, D_MODEL)
    gq, gk, la, gv, gate, sq, skb, svb, skf, svf = _project(x2d, bsz, pw, layer, kv_bufs, tm)
    r3 = lambda a: a.reshape(bsz, t, a.shape[-1])
    o_gla, s_new = _gla(r3(gq), r3(gk), r3(la), r3(gv), s0, layer, tg)
    o_sb = _sb(r3(sq), r3(skb), r3(svb), past_k, past_v, layer, tq, n_sub)
    y = _finish(x2d, o_gla.reshape(bsz * t, GLA_WIDTH), gate, o_sb.reshape(bsz * t, SB_WIDTH), pw, layer, tf)
    return y.reshape(bsz, t, D_MODEL), (skf, svf), s_new


def kernel(x_prompt, x_sample, cache_sb_k, cache_sb_v, state_gla, norm1_g, w_in, w_a2, b_a2, q_norm_g, k_norm_g,
           gla_norm_g, w_out, norm2_g, w_up, w_down):
    pw = _prepare_weights(norm1_g, w_in, w_a2, b_a2, q_norm_g, k_norm_g, gla_norm_g, w_out, norm2_g, w_up, w_down)
    past_k = jnp.transpose(cache_sb_k, (0, 1, 3, 4, 2))
    past_v = jnp.transpose(cache_sb_v, (0, 1, 3, 4, 2))
    xp, xs = x_prompt, x_sample
    kv_p, kv_s, states_p, states_s = None, None, [], []
    for layer in range(DEPTH):
        xp, kv_p, sp = _stream_layer(xp, pw, layer, kv_p, None, None, None, tm=512, tf=512, tg=1024, tq=128, n_sub=8)
        xs, kv_s, ss = _stream_layer(xs, pw, layer, kv_s, past_k, past_v, state_gla, tm=512, tf=256, tg=CHUNK,
                                     tq=CHUNK, n_sub=1)
        states_p.append(sp)
        states_s.append(ss)
    return (xp, xs, kv_p[0], kv_p[1], jnp.stack(states_p), kv_s[0], kv_s[1], jnp.stack(states_s))
```

```python
import functools

import jax
import jax.numpy as jnp
from jax import lax
from jax.experimental import pallas as pl
from jax.experimental.pallas import tpu as pltpu

D_MODEL = 1024
DEPTH = 2
CHUNK = 64
GLA_HEADS = 4
GLA_DK = 64
GLA_DV = 128
GLA_KEY = GLA_HEADS * GLA_DK
GLA_WIDTH = GLA_HEADS * GLA_DV
GATE_RANK = 16
GATE_NORM = 16.0
SB_HEADS = 8
SB_HD = 64
SB_WIDTH = SB_HEADS * SB_HD
D_FF = 4 * D_MODEL
EPS = 1e-6

LANES = 128
ROW_TILE = 16
SB_UNIT = 3
SB_ZERO_LOG = 110.0
LOG2E = 1.4426950408889634
VMEM_LIMIT = 56 * 1024 * 1024

_GATE_COLS = 2 * GLA_KEY + 2 * GLA_WIDTH
_COL_QK = (0, 2 * GLA_KEY)
_COL_GV = (2 * GLA_KEY, 2 * GLA_KEY + GLA_WIDTH)
_COL_GATE = (2 * GLA_KEY + GLA_WIDTH, _GATE_COLS)
_COL_ALR = (_GATE_COLS, _GATE_COLS + LANES)
GLA_COLS = _COL_ALR[1]
SB_COL0 = _GATE_COLS + GATE_RANK
_COL_SQ = (0, SB_WIDTH)
_COL_SK = (SB_WIDTH, 2 * SB_WIDTH)
_COL_SV = (2 * SB_WIDTH, 3 * SB_WIDTH)

F32 = jnp.float32
BF16 = jnp.bfloat16


def _dot(a, b):
    return jnp.dot(a, b, preferred_element_type=F32)


def _dot_nt(a, b):
    return lax.dot_general(a, b, (((1,), (1,)), ((), ())), preferred_element_type=F32)


def _iota(shape, dim):
    return lax.broadcasted_iota(jnp.int32, shape, dim)


def _split3(x):
    p1 = x.astype(BF16)
    r1 = x - p1.astype(F32)
    p2 = r1.astype(BF16)
    p3 = (r1 - p2.astype(F32)).astype(BF16)
    return p1, p2, p3


def _split2(x):
    p1 = x.astype(BF16)
    p2 = (x - p1.astype(F32)).astype(BF16)
    return p1, p2


def _softplus(z):
    return jnp.maximum(z, 0.0) + jnp.log(1.0 + jnp.exp(-jnp.abs(z)))


def _rmsnorm_rows(x, g):
    ms = jnp.mean(x * x, axis=-1, keepdims=True)
    return x * lax.rsqrt(ms + EPS) * g


def _project_kernel(*refs, aliased, n_split):
    x_ref, g1_ref, wgla_ref, wsb_ref, wa2_ref, ba2_ref, qg_ref, kg_ref = refs[:8]
    gq_ref, gk_ref, la_ref, gv_ref, gate_ref, sq_ref, skb_ref, svb_ref, k4_ref, v4_ref = refs[8 + 2 * aliased:]
    rows_per_split = x_ref.shape[0] // n_split

    r = _iota((LANES, LANES), 0) // SB_HD
    c = _iota((LANES, LANES), 1) // SB_HD
    ones_bd = jnp.where(r == c, 1.0, 0.0).astype(BF16)

    def head_mean_squares(s):
        out = []
        for j in range(SB_WIDTH // LANES):
            sj = s[:, j * LANES:(j + 1) * LANES]
            hi, lo = _split2(sj * sj)
            out.append((_dot(hi, ones_bd) + _dot(lo, ones_bd)) * (1.0 / SB_HD))
        return jnp.concatenate(out, axis=-1)

    def products(xn):
        proj = lambda w_ref, cols: _dot(xn, w_ref[:, cols[0]:cols[1]])
        qk = proj(wgla_ref, _COL_QK)
        gv = proj(wgla_ref, _COL_GV)
        gate = proj(wgla_ref, _COL_GATE)
        alr = proj(wgla_ref, _COL_ALR)
        sq = proj(wsb_ref, _COL_SQ)
        sk = proj(wsb_ref, _COL_SK)
        sv = proj(wsb_ref, _COL_SV)
        y = _dot(alr.astype(BF16), wa2_ref[...]) + ba2_ref[...]
        return qk, gv, gate, sq, sk, sv, y, head_mean_squares(sq), head_mean_squares(sk)

    def store_cache_layout(ref, row0, a):
        rows = ref.reshape(x_ref.shape[0] * SB_HEADS, SB_HD)
        for h in range(SB_HEADS):
            rows[pl.ds(row0 * SB_HEADS + h, a.shape[0], stride=SB_HEADS), :] = a[:, h * SB_HD:(h + 1) * SB_HD]

    def write(row0, qk, gv, gate, sq, sk, sv, y, ms_q, ms_k):
        rows = slice(row0, row0 + rows_per_split)
        gq_ref[rows, :] = qk[:, :GLA_KEY] * (GLA_DK ** -0.5)
        gk_ref[rows, :] = qk[:, GLA_KEY:]
        gv_ref[rows, :] = gv.astype(BF16)
        gate_ref[rows, :] = gate
        svb_ref[rows, :] = sv.astype(BF16)
        store_cache_layout(v4_ref, row0, sv)
        la_ref[rows, :] = (jnp.minimum(y, 0.0) - jnp.log(1.0 + jnp.exp(-jnp.abs(y)))) * (1.0 / GATE_NORM)
        sq_ref[rows, :] = (sq * lax.rsqrt(ms_q + EPS) * qg_ref[...] * (SB_HD ** -0.5)).astype(BF16)
        sk_n = sk * lax.rsqrt(ms_k + EPS) * kg_ref[...]
        skb_ref[rows, :] = sk_n.astype(BF16)
        store_cache_layout(k4_ref, row0, sk_n)

    row0s = [i * rows_per_split for i in range(n_split)]
    xns = [_rmsnorm_rows(x_ref[r0:r0 + rows_per_split, :], g1_ref[...]).astype(BF16) for r0 in row0s]
    results = [products(xn) for xn in xns]
    for r0, res in zip(row0s, results):
        write(r0, *res)


def _project(x2d, bsz, pw, layer, kv_bufs, tm):
    n = x2d.shape[0]
    t = n // bsz
    aliased = kv_bufs is not None
    row = lambda w: pl.BlockSpec((tm, w), lambda i: (i, 0))
    per_layer = lambda a: pl.BlockSpec((None,) + a.shape[1:], lambda i: (layer,) + (0,) * (a.ndim - 1))
    weights = (pw['g1'], pw['win'], pw['wsb'], pw['wa2'], pw['ba2'], pw['qg'], pw['kg'])
    weight_specs = [per_layer(w) for w in weights]
    weight_specs[1] = pl.BlockSpec((None, D_MODEL, GLA_COLS), lambda i: (layer, 0, 0))
    out_cols = ((GLA_KEY, F32), (GLA_KEY, F32), (GLA_KEY, F32), (GLA_WIDTH, BF16), (GLA_WIDTH, F32),
                (SB_WIDTH, BF16), (SB_WIDTH, BF16), (SB_WIDTH, BF16))
    if tm <= t:
        kv_spec = pl.BlockSpec((None, 1, tm, SB_HEADS, SB_HD), lambda i: (layer, i // (t // tm), i % (t // tm), 0, 0))
    else:
        kv_spec = pl.BlockSpec((None, tm // t, t, SB_HEADS, SB_HD), lambda i: (layer, i, 0, 0, 0))
    kv_shape = jax.ShapeDtypeStruct((DEPTH, bsz, t, SB_HEADS, SB_HD), F32)
    n_in = 1 + len(weights)
    return pl.pallas_call(
        functools.partial(_project_kernel, aliased=aliased, n_split=1),
        grid=(n // tm,),
        in_specs=[row(D_MODEL)] + weight_specs + ([pl.BlockSpec(memory_space=pl.ANY)] * 2 if aliased else []),
        out_specs=[row(w) for w, _ in out_cols] + [kv_spec, kv_spec],
        out_shape=[jax.ShapeDtypeStruct((n, w), dt) for w, dt in out_cols] + [kv_shape, kv_shape],
        input_output_aliases={n_in: len(out_cols), n_in + 1: len(out_cols) + 1} if aliased else {},
        compiler_params=pltpu.CompilerParams(dimension_semantics=("parallel",), vmem_limit_bytes=VMEM_LIMIT),
        name="project",
    )(x2d, *weights, *(kv_bufs if aliased else ()))


def _gla_kernel(*refs, n_chunks, chunks_per_step, has_init):
    if has_init:
        q_ref, k_ref, la_ref, v_ref, s0_ref, o_ref, sout_ref, s_scr, b_scr = refs
    else:
        q_ref, k_ref, la_ref, v_ref, o_ref, sout_ref, s_scr, b_scr = refs
    g = pl.program_id(2)

    @pl.when(g == 0)
    def _():
        if has_init:
            s_scr[...] = s0_ref[...].reshape(2 * GLA_DK, GLA_DV)
        else:
            s_scr[...] = jnp.zeros_like(s_scr)

    rr = _iota((CHUNK, CHUNK), 0)
    cc = _iota((CHUNK, CHUNK), 1)
    tri = jnp.where(cc <= rr, 1.0, 0.0).astype(BF16)
    r = _iota((2 * LANES, 2 * LANES), 0) // GLA_DK
    c = _iota((2 * LANES, 2 * LANES), 1) // GLA_DK
    ones_bd2 = jnp.where(r == c, 1.0, 0.0).astype(BF16)
    lane = _iota((CHUNK, LANES), 1)
    lane_t = lane % CHUNK
    tile_t = _iota((ROW_TILE, LANES), 1) % CHUNK
    srow = _iota((CHUNK, LANES), 0)
    head_a = lane < GLA_DK

    n_sub = CHUNK // ROW_TILE
    sub = lambda g: slice(g * ROW_TILE, (g + 1) * ROW_TILE)

    def operands(slot, base, b):
        b2 = b * LOG2E
        b_scr[slot] = b2
        rows = pl.ds(base, CHUNK)
        q = q_ref[rows, :]
        k = k_ref[rows, :]

        b_end = jnp.concatenate(
            [jnp.broadcast_to(b[(g + 1) * ROW_TILE - 1:(g + 1) * ROW_TILE], (ROW_TILE, LANES)) for g in range(n_sub)],
            axis=0)
        k_dec = (k * jnp.exp(b_end - b)).astype(BF16)
        q_dec_rhs = []
        for g in range(n_sub - 1):
            lo = (g + 1) * ROW_TILE
            q_dec = q[lo:] * jnp.exp(b[lo:] - b[lo - 1:lo])
            pad = jnp.zeros((lo, LANES), F32)
            in_a = _iota((CHUNK - lo, LANES), 1) < GLA_DK
            q_dec_rhs.append(
                jnp.concatenate([pad, jnp.where(in_a, q_dec, 0.0), pad, jnp.where(in_a, 0.0, q_dec)],
                                axis=0).astype(BF16))

        es = []
        for t in range(CHUNK):
            g = t // ROW_TILE
            brow = b_scr[slot, t:t + 1, :]
            qrow = q_ref[pl.ds(base + t, 1), :]
            es.append(jnp.exp2(brow - b2[sub(g)]).astype(BF16) * (qrow * k[sub(g)]).astype(BF16))
        diag_lhs = jnp.concatenate(
            [jnp.concatenate([es[2 * p], es[2 * p + 1]], axis=1) for p in range(CHUNK // 2)], axis=0)

        qe = q * jnp.exp(b)
        qe_a = jnp.where(head_a, qe, 0.0).astype(BF16)
        qe_b = jnp.where(head_a, 0.0, qe).astype(BF16)
        b_t = b.T
        b_last = b_t[:, CHUNK - 1:CHUNK]
        kd_t = (k.T * jnp.exp(b_last - b_t)).astype(BF16)
        return k_dec, q_dec_rhs, diag_lhs, qe_a, qe_b, kd_t, jnp.exp(b_last)

    def scores(off_tiles, red):
        pt_tiles = list(off_tiles) + [jnp.zeros((ROW_TILE, LANES), F32)]
        for t in range(CHUNK):
            g, p, half = t // ROW_TILE, t // 2, t % 2
            r_t = red[p * ROW_TILE:(p + 1) * ROW_TILE, half * LANES:(half + 1) * LANES]
            pt_tiles[g] = jnp.where(tile_t == t, r_t, pt_tiles[g])
        pt = jnp.concatenate(pt_tiles, axis=0)
        return jnp.where(srow <= lane_t, pt, 0.0)

    def step_body(si, carry):
        slots = range(chunks_per_step)
        bases = [pl.multiple_of((si * chunks_per_step + slot) * CHUNK, CHUNK) for slot in slots]
        bs = []
        for base in bases:
            p1, p2, p3 = _split3(la_ref[pl.ds(base, CHUNK), :])
            bs.append(_dot(tri, p1) + _dot(tri, p2) + _dot(tri, p3))
        ops = [operands(slot, bases[slot], bs[slot]) for slot in slots]
        v2s = [v_ref[pl.ds(base, CHUNK), :] for base in bases]
        offs = [[_dot_nt(op[0][sub(g)], op[1][g]) for g in range(n_sub - 1)] for op in ops]
        reds = [_dot(op[2], ones_bd2) for op in ops]
        upds = [_dot(op[5], v2) for op, v2 in zip(ops, v2s)]
        pts = [scores(off, red).T.astype(BF16) for off, red in zip(offs, reds)]
        o_intras = [_dot(pt, v2) for pt, v2 in zip(pts, v2s)]

        s = s_scr[...]
        for slot in slots:
            _, _, _, qe_a, qe_b, _, decay = ops[slot]
            rows = pl.ds(bases[slot], CHUNK)
            s_bf = s.astype(BF16)
            o_ref[rows, :GLA_DV] = o_intras[slot][:CHUNK, :GLA_DV] + _dot(qe_a, s_bf)
            o_ref[rows, GLA_DV:] = o_intras[slot][CHUNK:, GLA_DV:] + _dot(qe_b, s_bf)
            upd = upds[slot]
            s = decay * s + jnp.concatenate([upd[:GLA_DK, :GLA_DV], upd[GLA_DK:, GLA_DV:]], axis=0)
        s_scr[...] = s
        return carry

    lax.fori_loop(0, n_chunks // chunks_per_step, step_body, 0)

    @pl.when(g == pl.num_programs(2) - 1)
    def _():
        sout_ref[...] = s_scr[...].reshape(2, GLA_DK, GLA_DV)


def _gla(gq, gk, la, gv, s0, layer, tg):
    bsz, t, _ = gq.shape
    has_init = s0 is not None
    n_chunks = tg // CHUNK
    chunks_per_step = 16 if n_chunks % 16 == 0 else 1
    qspec = pl.BlockSpec((None, tg, LANES), lambda b, p, g: (b, g, p))
    vspec = pl.BlockSpec((None, tg, 2 * GLA_DV), lambda b, p, g: (b, g, p))
    sspec = pl.BlockSpec((None, 2, GLA_DK, GLA_DV), lambda b, p, g: (b, p, 0, 0))
    s0spec = pl.BlockSpec((None, None, 2, GLA_DK, GLA_DV), lambda b, p, g: (layer, b, p, 0, 0))
    in_specs = [qspec, qspec, qspec, vspec] + ([s0spec] if has_init else [])
    args = (gq, gk, la, gv) + ((s0,) if has_init else ())
    return pl.pallas_call(
        functools.partial(_gla_kernel, n_chunks=n_chunks, chunks_per_step=chunks_per_step, has_init=has_init),
        grid=(bsz, GLA_HEADS // 2, t // tg),
        in_specs=in_specs,
        out_specs=[vspec, sspec],
        out_shape=[jax.ShapeDtypeStruct((bsz, t, GLA_WIDTH), F32),
                   jax.ShapeDtypeStruct((bsz, GLA_HEADS, GLA_DK, GLA_DV), F32)],
        scratch_shapes=[pltpu.VMEM((2 * GLA_DK, GLA_DV), F32), pltpu.VMEM((chunks_per_step, CHUNK, LANES), F32)],
        compiler_params=pltpu.CompilerParams(
            dimension_semantics=("parallel", "parallel", "arbitrary"), vmem_limit_bytes=VMEM_LIMIT),
        name="gla",
    )(*args)


def _cumsum_matrix(tk):
    r = _iota((tk, 2 * tk), 0)
    c = _iota((tk, 2 * tk), 1)
    return jnp.where((c >= tk) | (r > c), 1.0, 0.0).astype(BF16)


def _sb_unit(streams):
    n_keys = lambda b: b[0].shape[1] if b[4] else b[0].shape[0]
    zs = [[_dot(q2, b[0]) if b[4] else _dot_nt(q2, b[0]) for b in blocks] for q2, blocks, _, _ in streams]
    sps = []
    for z_list, (_, blocks, _, _) in zip(zs, streams):
        sp_list = []
        for z, (_, _, _, mask, _) in zip(z_list, blocks):
            sp = _softplus(z)
            sp_list.append(sp if mask is None else jnp.where(mask, sp, 0.0))
        sps.append(sp_list)
    crs = []
    for sp_list, (q2, blocks, _, _) in zip(sps, streams):
        cr_list = [None] * len(blocks)
        rows = q2.shape[0]
        for tk in sorted({n_keys(b) for b in blocks}):
            idx = [n for n, b in enumerate(blocks) if n_keys(b) == tk]
            u = blocks[idx[0]][2]
            cr = _dot(jnp.concatenate([sp_list[n] for n in idx], axis=0).astype(BF16), u)
            for m, n in enumerate(idx):
                cr_list[n] = cr[m * rows:(m + 1) * rows]
        crs.append(cr_list)
    ws = []
    carries = []
    for z_list, sp_list, cr_list, (_, blocks, carry, _) in zip(zs, sps, crs, streams):
        w_list = []
        for z, sp, cr, block in zip(z_list, sp_list, cr_list, blocks):
            tk, mask = n_keys(block), block[3]
            w = jnp.exp(z - sp - cr[:, :tk] - carry[:, :tk])
            if mask is not None:
                w = jnp.where(mask, w, 0.0)
            w_list.append(w.astype(BF16))
            rs = cr[:, tk:]
            if tk < LANES:
                rs = jnp.concatenate([rs] * (LANES // tk), axis=1)
            carry = carry + rs
        ws.append(w_list)
        carries.append(carry)
    out = []
    for w_list, carry, (_, blocks, _, acc) in zip(ws, carries, streams):
        pv = None
        for w, (_, vj, _, _, transposed) in zip(w_list, blocks):
            d = _dot_nt(w, vj) if transposed else _dot(w, vj)
            pv = d if pv is None else pv + d
        out.append((carry, acc + pv))
    return out


def _sb_kernel(*refs, tq, n_sub, n_past_blocks):
    if n_past_blocks:
        q_ref, k_ref, v_ref, pk_ref, pv_ref, o_ref = refs
    else:
        q_ref, k_ref, v_ref, o_ref = refs
    i = pl.program_id(2)
    lane = _iota((tq, LANES), 1)
    u_own = _cumsum_matrix(tq)
    strict = _iota((2 * tq, tq), 1) < _iota((2 * tq, tq), 0) % tq
    zeros = jnp.zeros((2 * tq, LANES), F32)

    def stacked_heads(r):
        q = q_ref[r * tq:(r + 1) * tq, :]
        zero_q = jnp.zeros_like(q)
        return jnp.concatenate([jnp.where(lane < SB_HD, q, zero_q), jnp.where(lane < SB_HD, zero_q, q)], axis=0)

    q2s = [stacked_heads(r) for r in range(n_sub)]

    if n_past_blocks:
        u_past = _cumsum_matrix(LANES)

        def past_block(ref, j):
            return ref[:, :, j * LANES:(j + 1) * LANES].reshape(2 * SB_HD, LANES).astype(BF16)

        blocks = [(k_ref[...], v_ref[...], u_own, strict, False)]
        for j in reversed(range(n_past_blocks)):
            blocks.append((past_block(pk_ref, j), past_block(pv_ref, j), u_past, None, True))
        accs = [_sb_unit([(q2s[0], blocks, zeros, zeros)])[0][1]]
    else:
        def load_unit(r, n, first):
            blocks = []
            for kk in range(SB_UNIT):
                j = i * n_sub + r - n * SB_UNIT - kk
                rows = pl.ds(pl.multiple_of(jnp.maximum(j, 0) * tq, tq), tq)
                vj = v_ref[rows, :]
                diag = first and kk == 0
                if not diag:
                    vj = jnp.where(j >= 0, vj, jnp.zeros_like(vj))
                blocks.append((k_ref[rows, :], vj, u_own, strict if diag else None, False))
            return blocks

        def min_carry(state):
            m = state[0][0]
            for carry, _ in state[1:]:
                m = jnp.minimum(m, carry)
            return jnp.min(m)

        state = _sb_unit([(q2s[r], load_unit(r, 0, True), zeros, zeros) for r in range(n_sub)])
        n_units = (i * n_sub + n_sub - 1 + SB_UNIT) // SB_UNIT

        def cond(st):
            return (st[0] < n_units) & (st[1] < SB_ZERO_LOG)

        def body(st):
            n, _, state = st
            state = _sb_unit([(q2s[r], load_unit(r, n, False), *state[r]) for r in range(n_sub)])
            return n + 1, min_carry(state), state

        state = lax.while_loop(cond, body, (jnp.int32(1), min_carry(state), state))[2]
        accs = [acc for _, acc in state]
    for r, acc in enumerate(accs):
        o_ref[r * tq:(r + 1) * tq, :] = jnp.where(lane < SB_HD, acc[:tq], acc[tq:]).astype(o_ref.dtype)


def _sb(q, k, v, past_k, past_v, layer, tq, n_sub):
    bsz, t, _ = q.shape
    n_past_blocks = 0 if past_k is None else past_k.shape[-1] // LANES
    qspec = pl.BlockSpec((None, tq * n_sub, LANES), lambda b, p, i: (b, i, p))
    kspec = pl.BlockSpec((None, t, LANES), lambda b, p, i: (b, 0, p))
    in_specs = [qspec, kspec, kspec]
    args = (q, k, v)
    if n_past_blocks:
        pspec = pl.BlockSpec((None, None, 2, SB_HD, past_k.shape[-1]), lambda b, p, i: (layer, b, p, 0, 0))
        in_specs += [pspec, pspec]
        args += (past_k, past_v)
    return pl.pallas_call(
        functools.partial(_sb_kernel, tq=tq, n_sub=n_sub, n_past_blocks=n_past_blocks),
        grid=(bsz, SB_WIDTH // LANES, t // (tq * n_sub)),
        in_specs=in_specs,
        out_specs=qspec,
        out_shape=jax.ShapeDtypeStruct((bsz, t, SB_WIDTH), BF16),
        compiler_params=pltpu.CompilerParams(
            dimension_semantics=("parallel", "parallel", "arbitrary"), vmem_limit_bytes=VMEM_LIMIT),
        name="sb",
    )(*args)


def _finish_kernel(x_ref, og_ref, gate_ref, osb_ref, gg_ref, wo_ref, g2_ref, wup_ref, wdn_ref, y_ref, *, ff_block):
    og = og_ref[...]
    normed = [_rmsnorm_rows(og[:, h * GLA_DV:(h + 1) * GLA_DV], gg_ref[...]) for h in range(GLA_HEADS)]
    gate = gate_ref[...]
    mix_g = (jnp.concatenate(normed, axis=-1) * (gate * jax.nn.sigmoid(gate))).astype(BF16)
    h = x_ref[...] + _dot(mix_g, wo_ref[:GLA_WIDTH, :]) + _dot(osb_ref[...], wo_ref[GLA_WIDTH:, :])
    hn = _rmsnorm_rows(h, g2_ref[...]).astype(BF16)
    mlp = None
    for j in range(D_FF // ff_block):
        cols = slice(j * ff_block, (j + 1) * ff_block)
        u = _dot(hn, wup_ref[:, cols])
        a = jnp.square(jnp.maximum(u, 0.0)).astype(BF16)
        d = _dot(a, wdn_ref[cols, :])
        mlp = d if mlp is None else mlp + d
    y_ref[...] = h + mlp


def _finish(x2d, og, gate, osb, pw, layer, tm):
    n = x2d.shape[0]
    row = lambda w: pl.BlockSpec((tm, w), lambda i: (i, 0))
    per_layer = lambda a: pl.BlockSpec((None,) + a.shape[1:], lambda i: (layer,) + (0,) * (a.ndim - 1),
                                       pipeline_mode=pl.Buffered(1))
    weights = (pw['gg'], pw['wo'], pw['g2'], pw['wup'], pw['wdn'])
    return pl.pallas_call(
        functools.partial(_finish_kernel, ff_block=1024),
        grid=(n // tm,),
        in_specs=[row(D_MODEL), row(GLA_WIDTH), row(GLA_WIDTH), row(SB_WIDTH)] + [per_layer(w) for w in weights],
        out_specs=row(D_MODEL),
        out_shape=jax.ShapeDtypeStruct((n, D_MODEL), F32),
        compiler_params=pltpu.CompilerParams(dimension_semantics=("parallel",), vmem_limit_bytes=VMEM_LIMIT),
        name="finish",
    )(x2d, og, gate, osb, *weights)


def _prepare_weights(norm1_g, w_in, w_a2, b_a2, q_norm_g, k_norm_g, gla_norm_g, w_out, norm2_g, w_up, w_down):
    w_in = w_in.astype(BF16)
    row = lambda a: a.reshape(DEPTH, 1, a.shape[-1])
    return {
        'g1': row(norm1_g),
        'win': w_in,
        'wsb': w_in[:, :, SB_COL0:],
        'wa2': jnp.pad(w_a2, ((0, 0), (0, LANES - GATE_RANK), (0, 0))).astype(BF16),
        'ba2': row(b_a2),
        'qg': row(jnp.tile(q_norm_g, (1, SB_HEADS))),
        'kg': row(jnp.tile(k_norm_g, (1, SB_HEADS))),
        'gg': row(gla_norm_g),
        'wo': w_out.astype(BF16),
        'g2': row(norm2_g),
        'wup': w_up.astype(BF16),
        'wdn': w_down.astype(BF16),
    }


def _stream_layer(x, pw, layer, kv_bufs, past_k, past_v, s0, tm, tf, tg, tq, n_sub):
    bsz, t, _ = x.shape
    x2d = x.reshape(bsz * t, D_MODEL)
    gq, gk, la, gv, gate, sq, skb, svb, skf, svf = _project(x2d, bsz, pw, layer, kv_bufs, tm)
    r3 = lambda a: a.reshape(bsz, t, a.shape[-1])
    o_gla, s_new = _gla(r3(gq), r3(gk), r3(la), r3(gv), s0, layer, tg)
    o_sb = _sb(r3(sq), r3(skb), r3(svb), past_k, past_v, layer, tq, n_sub)
    y = _finish(x2d, o_gla.reshape(bsz * t, GLA_WIDTH), gate, o_sb.reshape(bsz * t, SB_WIDTH), pw, layer, tf)
    return y.reshape(bsz, t, D_MODEL), (skf, svf), s_new


def kernel(x_prompt, x_sample, cache_sb_k, cache_sb_v, state_gla, norm1_g, w_in, w_a2, b_a2, q_norm_g, k_norm_g,
           gla_norm_g, w_out, norm2_g, w_up, w_down):
    pw = _prepare_weights(norm1_g, w_in, w_a2, b_a2, q_norm_g, k_norm_g, gla_norm_g, w_out, norm2_g, w_up, w_down)
    past_k = jnp.transpose(cache_sb_k, (0, 1, 3, 4, 2))
    past_v = jnp.transpose(cache_sb_v, (0, 1, 3, 4, 2))
    xp, xs = x_prompt, x_sample
    kv_p, kv_s, states_p, states_s = None, None, [], []
    for layer in range(DEPTH):
        xp, kv_p, sp = _stream_layer(xp, pw, layer, kv_p, None, None, None, tm=512, tf=512, tg=1024, tq=128, n_sub=4)
        xs, kv_s, ss = _stream_layer(xs, pw, layer, kv_s, past_k, past_v, state_gla, tm=512, tf=256, tg=CHUNK,
                                     tq=CHUNK, n_sub=1)
        states_p.append(sp)
        states_s.append(ss)
    return (xp, xs, kv_p[0], kv_p[1], jnp.stack(states_p), kv_s[0], kv_s[1], jnp.stack(states_s))
```

```python
import functools

import jax
import jax.numpy as jnp
from jax import lax
from jax.experimental import pallas as pl
from jax.experimental.pallas import tpu as pltpu

D_MODEL = 1024
DEPTH = 2
CHUNK = 64
GLA_HEADS = 4
GLA_DK = 64
GLA_DV = 128
GLA_KEY = GLA_HEADS * GLA_DK
GLA_WIDTH = GLA_HEADS * GLA_DV
GATE_RANK = 16
GATE_NORM = 16.0
SB_HEADS = 8
SB_HD = 64
SB_WIDTH = SB_HEADS * SB_HD
D_FF = 4 * D_MODEL
EPS = 1e-6

LANES = 128
ROW_TILE = 16
SB_FIRST_UNIT = 3
SB_NEXT_UNIT = 1
SB_ZERO_LOG = 110.0
LOG2E = 1.4426950408889634
VMEM_LIMIT = 56 * 1024 * 1024

_GATE_COLS = 2 * GLA_KEY + 2 * GLA_WIDTH
_COL_QK = (0, 2 * GLA_KEY)
_COL_GV = (2 * GLA_KEY, 2 * GLA_KEY + GLA_WIDTH)
_COL_GATE = (2 * GLA_KEY + GLA_WIDTH, _GATE_COLS)
_COL_ALR = (_GATE_COLS, _GATE_COLS + LANES)
GLA_COLS = _COL_ALR[1]
SB_COL0 = _GATE_COLS + GATE_RANK
_COL_SQ = (0, SB_WIDTH)
_COL_SK = (SB_WIDTH, 2 * SB_WIDTH)
_COL_SV = (2 * SB_WIDTH, 3 * SB_WIDTH)

F32 = jnp.float32
BF16 = jnp.bfloat16


def _dot(a, b):
    return jnp.dot(a, b, preferred_element_type=F32)


def _dot_nt(a, b):
    return lax.dot_general(a, b, (((1,), (1,)), ((), ())), preferred_element_type=F32)


def _iota(shape, dim):
    return lax.broadcasted_iota(jnp.int32, shape, dim)


def _split3(x):
    p1 = x.astype(BF16)
    r1 = x - p1.astype(F32)
    p2 = r1.astype(BF16)
    p3 = (r1 - p2.astype(F32)).astype(BF16)
    return p1, p2, p3


def _split2(x):
    p1 = x.astype(BF16)
    p2 = (x - p1.astype(F32)).astype(BF16)
    return p1, p2


def _softplus(z):
    return jnp.maximum(z, 0.0) + jnp.log(1.0 + jnp.exp(-jnp.abs(z)))


def _rmsnorm_rows(x, g):
    ms = jnp.mean(x * x, axis=-1, keepdims=True)
    return x * lax.rsqrt(ms + EPS) * g


def _project_kernel(*refs, aliased, n_split):
    x_ref, g1_ref, wgla_ref, wsb_ref, wa2_ref, ba2_ref, qg_ref, kg_ref = refs[:8]
    gq_ref, gk_ref, la_ref, gv_ref, gate_ref, sq_ref, skb_ref, svb_ref, k4_ref, v4_ref = refs[8 + 2 * aliased:]
    rows_per_split = x_ref.shape[0] // n_split

    r = _iota((LANES, LANES), 0) // SB_HD
    c = _iota((LANES, LANES), 1) // SB_HD
    ones_bd = jnp.where(r == c, 1.0, 0.0).astype(BF16)

    def head_mean_squares(s):
        out = []
        for j in range(SB_WIDTH // LANES):
            sj = s[:, j * LANES:(j + 1) * LANES]
            hi, lo = _split2(sj * sj)
            out.append((_dot(hi, ones_bd) + _dot(lo, ones_bd)) * (1.0 / SB_HD))
        return jnp.concatenate(out, axis=-1)

    def products(xn):
        proj = lambda w_ref, cols: _dot(xn, w_ref[:, cols[0]:cols[1]])
        qk = proj(wgla_ref, _COL_QK)
        gv = proj(wgla_ref, _COL_GV)
        gate = proj(wgla_ref, _COL_GATE)
        alr = proj(wgla_ref, _COL_ALR)
        sq = proj(wsb_ref, _COL_SQ)
        sk = proj(wsb_ref, _COL_SK)
        sv = proj(wsb_ref, _COL_SV)
        y = _dot(alr.astype(BF16), wa2_ref[...]) + ba2_ref[...]
        return qk, gv, gate, sq, sk, sv, y, head_mean_squares(sq), head_mean_squares(sk)

    def store_cache_layout(ref, row0, a):
        rows = ref.reshape(x_ref.shape[0] * SB_HEADS, SB_HD)
        for h in range(SB_HEADS):
            rows[pl.ds(row0 * SB_HEADS + h, a.shape[0], stride=SB_HEADS), :] = a[:, h * SB_HD:(h + 1) * SB_HD]

    def write(row0, qk, gv, gate, sq, sk, sv, y, ms_q, ms_k):
        rows = slice(row0, row0 + rows_per_split)
        gq_ref[rows, :] = qk[:, :GLA_KEY] * (GLA_DK ** -0.5)
        gk_ref[rows, :] = qk[:, GLA_KEY:]
        gv_ref[rows, :] = gv.astype(BF16)
        gate_ref[rows, :] = gate
        svb_ref[rows, :] = sv.astype(BF16)
        store_cache_layout(v4_ref, row0, sv)
        la_ref[rows, :] = (jnp.minimum(y, 0.0) - jnp.log(1.0 + jnp.exp(-jnp.abs(y)))) * (1.0 / GATE_NORM)
        sq_ref[rows, :] = (sq * lax.rsqrt(ms_q + EPS) * qg_ref[...] * (SB_HD ** -0.5)).astype(BF16)
        sk_n = sk * lax.rsqrt(ms_k + EPS) * kg_ref[...]
        skb_ref[rows, :] = sk_n.astype(BF16)
        store_cache_layout(k4_ref, row0, sk_n)

    row0s = [i * rows_per_split for i in range(n_split)]
    xns = [_rmsnorm_rows(x_ref[r0:r0 + rows_per_split, :], g1_ref[...]).astype(BF16) for r0 in row0s]
    results = [products(xn) for xn in xns]
    for r0, res in zip(row0s, results):
        write(r0, *res)


def _project(x2d, bsz, pw, layer, kv_bufs, tm):
    n = x2d.shape[0]
    t = n // bsz
    aliased = kv_bufs is not None
    row = lambda w: pl.BlockSpec((tm, w), lambda i: (i, 0))
    per_layer = lambda a: pl.BlockSpec((None,) + a.shape[1:], lambda i: (layer,) + (0,) * (a.ndim - 1))
    weights = (pw['g1'], pw['win'], pw['wsb'], pw['wa2'], pw['ba2'], pw['qg'], pw['kg'])
    weight_specs = [per_layer(w) for w in weights]
    weight_specs[1] = pl.BlockSpec((None, D_MODEL, GLA_COLS), lambda i: (layer, 0, 0))
    out_cols = ((GLA_KEY, F32), (GLA_KEY, F32), (GLA_KEY, F32), (GLA_WIDTH, BF16), (GLA_WIDTH, F32),
                (SB_WIDTH, BF16), (SB_WIDTH, BF16), (SB_WIDTH, BF16))
    if tm <= t:
        kv_spec = pl.BlockSpec((None, 1, tm, SB_HEADS, SB_HD), lambda i: (layer, i // (t // tm), i % (t // tm), 0, 0))
    else:
        kv_spec = pl.BlockSpec((None, tm // t, t, SB_HEADS, SB_HD), lambda i: (layer, i, 0, 0, 0))
    kv_shape = jax.ShapeDtypeStruct((DEPTH, bsz, t, SB_HEADS, SB_HD), F32)
    n_in = 1 + len(weights)
    return pl.pallas_call(
        functools.partial(_project_kernel, aliased=aliased, n_split=1),
        grid=(n // tm,),
        in_specs=[row(D_MODEL)] + weight_specs + ([pl.BlockSpec(memory_space=pl.ANY)] * 2 if aliased else []),
        out_specs=[row(w) for w, _ in out_cols] + [kv_spec, kv_spec],
        out_shape=[jax.ShapeDtypeStruct((n, w), dt) for w, dt in out_cols] + [kv_shape, kv_shape],
        input_output_aliases={n_in: len(out_cols), n_in + 1: len(out_cols) + 1} if aliased else {},
        compiler_params=pltpu.CompilerParams(dimension_semantics=("parallel",), vmem_limit_bytes=VMEM_LIMIT),
        name="project",
    )(x2d, *weights, *(kv_bufs if aliased else ()))


def _gla_kernel(*refs, n_chunks, chunks_per_step, has_init):
    if has_init:
        q_ref, k_ref, la_ref, v_ref, s0_ref, o_ref, sout_ref, s_scr, b_scr = refs
    else:
        q_ref, k_ref, la_ref, v_ref, o_ref, sout_ref, s_scr, b_scr = refs
    g = pl.program_id(2)

    @pl.when(g == 0)
    def _():
        if has_init:
            s_scr[...] = s0_ref[...].reshape(2 * GLA_DK, GLA_DV)
        else:
            s_scr[...] = jnp.zeros_like(s_scr)

    rr = _iota((CHUNK, CHUNK), 0)
    cc = _iota((CHUNK, CHUNK), 1)
    tri = jnp.where(cc <= rr, 1.0, 0.0).astype(BF16)
    r = _iota((2 * LANES, 2 * LANES), 0) // GLA_DK
    c = _iota((2 * LANES, 2 * LANES), 1) // GLA_DK
    ones_bd2 = jnp.where(r == c, 1.0, 0.0).astype(BF16)
    lane = _iota((CHUNK, LANES), 1)
    lane_t = lane % CHUNK
    tile_t = _iota((ROW_TILE, LANES), 1) % CHUNK
    srow = _iota((CHUNK, LANES), 0)
    head_a = lane < GLA_DK

    n_sub = CHUNK // ROW_TILE
    sub = lambda g: slice(g * ROW_TILE, (g + 1) * ROW_TILE)

    def operands(slot, base, b):
        b2 = b * LOG2E
        b_scr[slot] = b2
        rows = pl.ds(base, CHUNK)
        q = q_ref[rows, :]
        k = k_ref[rows, :]

        b_end = jnp.concatenate(
            [jnp.broadcast_to(b[(g + 1) * ROW_TILE - 1:(g + 1) * ROW_TILE], (ROW_TILE, LANES)) for g in range(n_sub)],
            axis=0)
        k_dec = (k * jnp.exp(b_end - b)).astype(BF16)
        q_dec_rhs = []
        for g in range(n_sub - 1):
            lo = (g + 1) * ROW_TILE
            q_dec = q[lo:] * jnp.exp(b[lo:] - b[lo - 1:lo])
            pad = jnp.zeros((lo, LANES), F32)
            in_a = _iota((CHUNK - lo, LANES), 1) < GLA_DK
            q_dec_rhs.append(
                jnp.concatenate([pad, jnp.where(in_a, q_dec, 0.0), pad, jnp.where(in_a, 0.0, q_dec)],
                                axis=0).astype(BF16))

        es = []
        for t in range(CHUNK):
            g = t // ROW_TILE
            brow = b_scr[slot, t:t + 1, :]
            qrow = q_ref[pl.ds(base + t, 1), :]
            es.append(jnp.exp2(brow - b2[sub(g)]).astype(BF16) * (qrow * k[sub(g)]).astype(BF16))
        diag_lhs = jnp.concatenate(
            [jnp.concatenate([es[2 * p], es[2 * p + 1]], axis=1) for p in range(CHUNK // 2)], axis=0)

        qe = q * jnp.exp(b)
        qe_a = jnp.where(head_a, qe, 0.0).astype(BF16)
        qe_b = jnp.where(head_a, 0.0, qe).astype(BF16)
        b_t = b.T
        b_last = b_t[:, CHUNK - 1:CHUNK]
        kd_t = (k.T * jnp.exp(b_last - b_t)).astype(BF16)
        return k_dec, q_dec_rhs, diag_lhs, qe_a, qe_b, kd_t, jnp.exp(b_last)

    def scores(off_tiles, red):
        pt_tiles = list(off_tiles) + [jnp.zeros((ROW_TILE, LANES), F32)]
        for t in range(CHUNK):
            g, p, half = t // ROW_TILE, t // 2, t % 2
            r_t = red[p * ROW_TILE:(p + 1) * ROW_TILE, half * LANES:(half + 1) * LANES]
            pt_tiles[g] = jnp.where(tile_t == t, r_t, pt_tiles[g])
        pt = jnp.concatenate(pt_tiles, axis=0)
        return jnp.where(srow <= lane_t, pt, 0.0)

    def step_body(si, carry):
        slots = range(chunks_per_step)
        bases = [pl.multiple_of((si * chunks_per_step + slot) * CHUNK, CHUNK) for slot in slots]
        bs = []
        for base in bases:
            p1, p2, p3 = _split3(la_ref[pl.ds(base, CHUNK), :])
            bs.append(_dot(tri, p1) + _dot(tri, p2) + _dot(tri, p3))
        ops = [operands(slot, bases[slot], bs[slot]) for slot in slots]
        v2s = [v_ref[pl.ds(base, CHUNK), :] for base in bases]
        offs = [[_dot_nt(op[0][sub(g)], op[1][g]) for g in range(n_sub - 1)] for op in ops]
        reds = [_dot(op[2], ones_bd2) for op in ops]
        upds = [_dot(op[5], v2) for op, v2 in zip(ops, v2s)]
        pts = [scores(off, red).T.astype(BF16) for off, red in zip(offs, reds)]
        o_intras = [_dot(pt, v2) for pt, v2 in zip(pts, v2s)]

        s = s_scr[...]
        for slot in slots:
            _, _, _, qe_a, qe_b, _, decay = ops[slot]
            rows = pl.ds(bases[slot], CHUNK)
            s_bf = s.astype(BF16)
            o_ref[rows, :GLA_DV] = o_intras[slot][:CHUNK, :GLA_DV] + _dot(qe_a, s_bf)
            o_ref[rows, GLA_DV:] = o_intras[slot][CHUNK:, GLA_DV:] + _dot(qe_b, s_bf)
            upd = upds[slot]
            s = decay * s + jnp.concatenate([upd[:GLA_DK, :GLA_DV], upd[GLA_DK:, GLA_DV:]], axis=0)
        s_scr[...] = s
        return carry

    lax.fori_loop(0, n_chunks // chunks_per_step, step_body, 0)

    @pl.when(g == pl.num_programs(2) - 1)
    def _():
        sout_ref[...] = s_scr[...].reshape(2, GLA_DK, GLA_DV)


def _gla(gq, gk, la, gv, s0, layer, tg):
    bsz, t, _ = gq.shape
    has_init = s0 is not None
    n_chunks = tg // CHUNK
    chunks_per_step = 16 if n_chunks % 16 == 0 else 1
    qspec = pl.BlockSpec((None, tg, LANES), lambda b, p, g: (b, g, p))
    vspec = pl.BlockSpec((None, tg, 2 * GLA_DV), lambda b, p, g: (b, g, p))
    sspec = pl.BlockSpec((None, 2, GLA_DK, GLA_DV), lambda b, p, g: (b, p, 0, 0))
    s0spec = pl.BlockSpec((None, None, 2, GLA_DK, GLA_DV), lambda b, p, g: (layer, b, p, 0, 0))
    in_specs = [qspec, qspec, qspec, vspec] + ([s0spec] if has_init else [])
    args = (gq, gk, la, gv) + ((s0,) if has_init else ())
    return pl.pallas_call(
        functools.partial(_gla_kernel, n_chunks=n_chunks, chunks_per_step=chunks_per_step, has_init=has_init),
        grid=(bsz, GLA_HEADS // 2, t // tg),
        in_specs=in_specs,
        out_specs=[vspec, sspec],
        out_shape=[jax.ShapeDtypeStruct((bsz, t, GLA_WIDTH), F32),
                   jax.ShapeDtypeStruct((bsz, GLA_HEADS, GLA_DK, GLA_DV), F32)],
        scratch_shapes=[pltpu.VMEM((2 * GLA_DK, GLA_DV), F32), pltpu.VMEM((chunks_per_step, CHUNK, LANES), F32)],
        compiler_params=pltpu.CompilerParams(
            dimension_semantics=("parallel", "parallel", "arbitrary"), vmem_limit_bytes=VMEM_LIMIT),
        name="gla",
    )(*args)


def _cumsum_matrix(tk):
    r = _iota((tk, 2 * tk), 0)
    c = _iota((tk, 2 * tk), 1)
    return jnp.where((c >= tk) | (r > c), 1.0, 0.0).astype(BF16)


def _sb_unit(streams):
    n_keys = lambda b: b[0].shape[1] if b[4] else b[0].shape[0]
    zs = [[_dot(q2, b[0]) if b[4] else _dot_nt(q2, b[0]) for b in blocks] for q2, blocks, _, _ in streams]
    sps = []
    for z_list, (_, blocks, _, _) in zip(zs, streams):
        sp_list = []
        for z, (_, _, _, mask, _) in zip(z_list, blocks):
            sp = _softplus(z)
            sp_list.append(sp if mask is None else jnp.where(mask, sp, 0.0))
        sps.append(sp_list)
    crs = []
    for sp_list, (q2, blocks, _, _) in zip(sps, streams):
        cr_list = [None] * len(blocks)
        rows = q2.shape[0]
        for tk in sorted({n_keys(b) for b in blocks}):
            idx = [n for n, b in enumerate(blocks) if n_keys(b) == tk]
            u = blocks[idx[0]][2]
            cr = _dot(jnp.concatenate([sp_list[n] for n in idx], axis=0).astype(BF16), u)
            for m, n in enumerate(idx):
                cr_list[n] = cr[m * rows:(m + 1) * rows]
        crs.append(cr_list)
    ws = []
    carries = []
    for z_list, sp_list, cr_list, (_, blocks, carry, _) in zip(zs, sps, crs, streams):
        w_list = []
        for z, sp, cr, block in zip(z_list, sp_list, cr_list, blocks):
            tk, mask = n_keys(block), block[3]
            w = jnp.exp(z - sp - cr[:, :tk] - carry[:, :tk])
            if mask is not None:
                w = jnp.where(mask, w, 0.0)
            w_list.append(w.astype(BF16))
            rs = cr[:, tk:]
            if tk < LANES:
                rs = jnp.concatenate([rs] * (LANES // tk), axis=1)
            carry = carry + rs
        ws.append(w_list)
        carries.append(carry)
    out = []
    for w_list, carry, (_, blocks, _, acc) in zip(ws, carries, streams):
        pv = None
        for w, (_, vj, _, _, transposed) in zip(w_list, blocks):
            d = _dot_nt(w, vj) if transposed else _dot(w, vj)
            pv = d if pv is None else pv + d
        out.append((carry, acc + pv))
    return out


def _sb_kernel(*refs, tq, n_sub, n_past_blocks):
    if n_past_blocks:
        q_ref, k_ref, v_ref, pk_ref, pv_ref, o_ref = refs
    else:
        q_ref, k_ref, v_ref, o_ref = refs
    i = pl.program_id(2)
    lane = _iota((tq, LANES), 1)
    u_own = _cumsum_matrix(tq)
    strict = _iota((2 * tq, tq), 1) < _iota((2 * tq, tq), 0) % tq
    zeros = jnp.zeros((2 * tq, LANES), F32)

    def stacked_heads(r):
        q = q_ref[r * tq:(r + 1) * tq, :]
        zero_q = jnp.zeros_like(q)
        return jnp.concatenate([jnp.where(lane < SB_HD, q, zero_q), jnp.where(lane < SB_HD, zero_q, q)], axis=0)

    q2s = [stacked_heads(r) for r in range(n_sub)]

    if n_past_blocks:
        u_past = _cumsum_matrix(LANES)

        def past_block(ref, j):
            return ref[:, :, j * LANES:(j + 1) * LANES].reshape(2 * SB_HD, LANES).astype(BF16)

        blocks = [(k_ref[...], v_ref[...], u_own, strict, False)]
        for j in reversed(range(n_past_blocks)):
            blocks.append((past_block(pk_ref, j), past_block(pv_ref, j), u_past, None, True))
        accs = [_sb_unit([(q2s[0], blocks, zeros, zeros)])[0][1]]
    else:
        def load_unit(r, n, first):
            blocks = []
            for kk in range(SB_FIRST_UNIT if first else SB_NEXT_UNIT):
                j = i * n_sub + r - kk - (0 if first else SB_FIRST_UNIT + (n - 1) * SB_NEXT_UNIT)
                rows = pl.ds(pl.multiple_of(jnp.maximum(j, 0) * tq, tq), tq)
                vj = v_ref[rows, :]
                diag = first and kk == 0
                if not diag:
                    vj = jnp.where(j >= 0, vj, jnp.zeros_like(vj))
                blocks.append((k_ref[rows, :], vj, u_own, strict if diag else None, False))
            return blocks

        def min_carry(state):
            m = state[0][0]
            for carry, _ in state[1:]:
                m = jnp.minimum(m, carry)
            return jnp.min(m)

        state = _sb_unit([(q2s[r], load_unit(r, 0, True), zeros, zeros) for r in range(n_sub)])
        blocks_left = jnp.maximum(i * n_sub + n_sub - SB_FIRST_UNIT, 0)
        n_units = 1 + (blocks_left + SB_NEXT_UNIT - 1) // SB_NEXT_UNIT

        def cond(st):
            return (st[0] < n_units) & (st[1] < SB_ZERO_LOG)

        def body(st):
            n, _, state = st
            state = _sb_unit([(q2s[r], load_unit(r, n, False), *state[r]) for r in range(n_sub)])
            return n + 1, min_carry(state), state

        state = lax.while_loop(cond, body, (jnp.int32(1), min_carry(state), state))[2]
        accs = [acc for _, acc in state]
    for r, acc in enumerate(accs):
        o_ref[r * tq:(r + 1) * tq, :] = jnp.where(lane < SB_HD, acc[:tq], acc[tq:]).astype(o_ref.dtype)


def _sb(q, k, v, past_k, past_v, layer, tq, n_sub):
    bsz, t, _ = q.shape
    n_past_blocks = 0 if past_k is None else past_k.shape[-1] // LANES
    qspec = pl.BlockSpec((None, tq * n_sub, LANES), lambda b, p, i: (b, i, p))
    kspec = pl.BlockSpec((None, t, LANES), lambda b, p, i: (b, 0, p))
    in_specs = [qspec, kspec, kspec]
    args = (q, k, v)
    if n_past_blocks:
        pspec = pl.BlockSpec((None, None, 2, SB_HD, past_k.shape[-1]), lambda b, p, i: (layer, b, p, 0, 0))
        in_specs += [pspec, pspec]
        args += (past_k, past_v)
    return pl.pallas_call(
        functools.partial(_sb_kernel, tq=tq, n_sub=n_sub, n_past_blocks=n_past_blocks),
        grid=(bsz, SB_WIDTH // LANES, t // (tq * n_sub)),
        in_specs=in_specs,
        out_specs=qspec,
        out_shape=jax.ShapeDtypeStruct((bsz, t, SB_WIDTH), BF16),
        compiler_params=pltpu.CompilerParams(
            dimension_semantics=("parallel", "parallel", "arbitrary"), vmem_limit_bytes=VMEM_LIMIT),
        name="sb",
    )(*args)


def _finish_kernel(x_ref, og_ref, gate_ref, osb_ref, gg_ref, wo_ref, g2_ref, wup_ref, wdn_ref, y_ref, *, ff_block):
    og = og_ref[...]
    normed = [_rmsnorm_rows(og[:, h * GLA_DV:(h + 1) * GLA_DV], gg_ref[...]) for h in range(GLA_HEADS)]
    gate = gate_ref[...]
    mix_g = (jnp.concatenate(normed, axis=-1) * (gate * jax.nn.sigmoid(gate))).astype(BF16)
    h = x_ref[...] + _dot(mix_g, wo_ref[:GLA_WIDTH, :]) + _dot(osb_ref[...], wo_ref[GLA_WIDTH:, :])
    hn = _rmsnorm_rows(h, g2_ref[...]).astype(BF16)
    mlp = None
    for j in range(D_FF // ff_block):
        cols = slice(j * ff_block, (j + 1) * ff_block)
        u = _dot(hn, wup_ref[:, cols])
        a = jnp.square(jnp.maximum(u, 0.0)).astype(BF16)
        d = _dot(a, wdn_ref[cols, :])
        mlp = d if mlp is None else mlp + d
    y_ref[...] = h + mlp


def _finish(x2d, og, gate, osb, pw, layer, tm):
    n = x2d.shape[0]
    row = lambda w: pl.BlockSpec((tm, w), lambda i: (i, 0))
    per_layer = lambda a: pl.BlockSpec((None,) + a.shape[1:], lambda i: (layer,) + (0,) * (a.ndim - 1),
                                       pipeline_mode=pl.Buffered(1))
    weights = (pw['gg'], pw['wo'], pw['g2'], pw['wup'], pw['wdn'])
    return pl.pallas_call(
        functools.partial(_finish_kernel, ff_block=1024),
        grid=(n // tm,),
        in_specs=[row(D_MODEL), row(GLA_WIDTH), row(GLA_WIDTH), row(SB_WIDTH)] + [per_layer(w) for w in weights],
        out_specs=row(D_MODEL),
        out_shape=jax.ShapeDtypeStruct((n, D_MODEL), F32),
        compiler_params=pltpu.CompilerParams(dimension_semantics=("parallel",), vmem_limit_bytes=VMEM_LIMIT),
        name="finish",
    )(x2d, og, gate, osb, *weights)


def _prepare_weights(norm1_g, w_in, w_a2, b_a2, q_norm_g, k_norm_g, gla_norm_g, w_out, norm2_g, w_up, w_down):
    w_in = w_in.astype(BF16)
    row = lambda a: a.reshape(DEPTH, 1, a.shape[-1])
    return {
        'g1': row(norm1_g),
        'win': w_in,
        'wsb': w_in[:, :, SB_COL0:],
        'wa2': jnp.pad(w_a2, ((0, 0), (0, LANES - GATE_RANK), (0, 0))).astype(BF16),
        'ba2': row(b_a2),
        'qg': row(jnp.tile(q_norm_g, (1, SB_HEADS))),
        'kg': row(jnp.tile(k_norm_g, (1, SB_HEADS))),
        'gg': row(gla_norm_g),
        'wo': w_out.astype(BF16),
        'g2': row(norm2_g),
        'wup': w_up.astype(BF16),
        'wdn': w_down.astype(BF16),
    }


def _stream_layer(x, pw, layer, kv_bufs, past_k, past_v, s0, tm, tf, tg, tq, n_sub):
    bsz, t, _ = x.shape
    x2d = x.reshape(bsz * t, D_MODEL)
    gq, gk, la, gv, gate, sq, skb, svb, skf, svf = _project(x2d, bsz, pw, layer, kv_bufs, tm)
    r3 = lambda a: a.reshape(bsz, t, a.shape[-1])
    o_gla, s_new = _gla(r3(gq), r3(gk), r3(la), r3(gv), s0, layer, tg)
    o_sb = _sb(r3(sq), r3(skb), r3(svb), past_k, past_v, layer, tq, n_sub)
    y = _finish(x2d, o_gla.reshape(bsz * t, GLA_WIDTH), gate, o_sb.reshape(bsz * t, SB_WIDTH), pw, layer, tf)
    return y.reshape(bsz, t, D_MODEL), (skf, svf), s_new


def kernel(x_prompt, x_sample, cache_sb_k, cache_sb_v, state_gla, norm1_g, w_in, w_a2, b_a2, q_norm_g, k_norm_g,
           gla_norm_g, w_out, norm2_g, w_up, w_down):
    pw = _prepare_weights(norm1_g, w_in, w_a2, b_a2, q_norm_g, k_norm_g, gla_norm_g, w_out, norm2_g, w_up, w_down)
    past_k = jnp.transpose(cache_sb_k, (0, 1, 3, 4, 2))
    past_v = jnp.transpose(cache_sb_v, (0, 1, 3, 4, 2))
    xp, xs = x_prompt, x_sample
    kv_p, kv_s, states_p, states_s = None, None, [], []
    for layer in range(DEPTH):
        xp, kv_p, sp = _stream_layer(xp, pw, layer, kv_p, None, None, None, tm=512, tf=512, tg=1024, tq=128, n_sub=4)
        xs, kv_s, ss = _stream_layer(xs, pw, layer, kv_s, past_k, past_v, state_gla, tm=512, tf=256, tg=CHUNK,
                                     tq=CHUNK, n_sub=1)
        states_p.append(sp)
        states_s.append(ss)
    return (xp, xs, kv_p[0], kv_p[1], jnp.stack(states_p), kv_s[0], kv_s[1], jnp.stack(states_s))
```

```python
import functools

import jax
import jax.numpy as jnp
from jax import lax
from jax.experimental import pallas as pl
from jax.experimental.pallas import tpu as pltpu

D_MODEL = 1024
DEPTH = 2
CHUNK = 64
GLA_HEADS = 4
GLA_DK = 64
GLA_DV = 128
GLA_KEY = GLA_HEADS * GLA_DK
GLA_WIDTH = GLA_HEADS * GLA_DV
GATE_RANK = 16
GATE_NORM = 16.0
SB_HEADS = 8
SB_HD = 64
SB_WIDTH = SB_HEADS * SB_HD
D_FF = 4 * D_MODEL
EPS = 1e-6

LANES = 128
ROW_TILE = 16
SB_FIRST_UNIT = 3
SB_NEXT_UNIT = 1
SB_ZERO_LOG = 110.0
LOG2E = 1.4426950408889634
VMEM_LIMIT = 56 * 1024 * 1024

_GATE_COLS = 2 * GLA_KEY + 2 * GLA_WIDTH
_COL_QK = (0, 2 * GLA_KEY)
_COL_GV = (2 * GLA_KEY, 2 * GLA_KEY + GLA_WIDTH)
_COL_GATE = (2 * GLA_KEY + GLA_WIDTH, _GATE_COLS)
_COL_ALR = (_GATE_COLS, _GATE_COLS + LANES)
GLA_COLS = _COL_ALR[1]
SB_COL0 = _GATE_COLS + GATE_RANK
_COL_SQ = (0, SB_WIDTH)
_COL_SK = (SB_WIDTH, 2 * SB_WIDTH)
_COL_SV = (2 * SB_WIDTH, 3 * SB_WIDTH)

F32 = jnp.float32
BF16 = jnp.bfloat16


def _dot(a, b):
    return jnp.dot(a, b, preferred_element_type=F32)


def _dot_nt(a, b):
    return lax.dot_general(a, b, (((1,), (1,)), ((), ())), preferred_element_type=F32)


def _iota(shape, dim):
    return lax.broadcasted_iota(jnp.int32, shape, dim)


def _split3(x):
    p1 = x.astype(BF16)
    r1 = x - p1.astype(F32)
    p2 = r1.astype(BF16)
    p3 = (r1 - p2.astype(F32)).astype(BF16)
    return p1, p2, p3


def _softplus(z):
    return jnp.maximum(z, 0.0) + jnp.log(1.0 + jnp.exp(-jnp.abs(z)))


def _rmsnorm_rows(x, g):
    ms = jnp.mean(x * x, axis=-1, keepdims=True)
    return x * lax.rsqrt(ms + EPS) * g


def _project_kernel(*refs, aliased, n_split):
    x_ref, g1_ref, wgla_ref, wsb_ref, wa2_ref, ba2_ref, qg_ref, kg_ref = refs[:8]
    gq_ref, gk_ref, la_ref, gv_ref, gate_ref, sq_ref, skb_ref, svb_ref, k4_ref, v4_ref = refs[8 + 2 * aliased:]
    rows_per_split = x_ref.shape[0] // n_split

    r = _iota((LANES, LANES), 0) // SB_HD
    c = _iota((LANES, LANES), 1) // SB_HD
    ones_bd = jnp.where(r == c, 1.0, 0.0).astype(BF16)

    def head_mean_squares(s):
        out = []
        for j in range(SB_WIDTH // LANES):
            sj = s[:, j * LANES:(j + 1) * LANES]
            out.append(_dot((sj * sj).astype(BF16), ones_bd) * (1.0 / SB_HD))
        return jnp.concatenate(out, axis=-1)

    def products(xn):
        proj = lambda w_ref, cols: _dot(xn, w_ref[:, cols[0]:cols[1]])
        qk = proj(wgla_ref, _COL_QK)
        gv = proj(wgla_ref, _COL_GV)
        gate = proj(wgla_ref, _COL_GATE)
        alr = proj(wgla_ref, _COL_ALR)
        sq = proj(wsb_ref, _COL_SQ)
        sk = proj(wsb_ref, _COL_SK)
        sv = proj(wsb_ref, _COL_SV)
        y = _dot(alr.astype(BF16), wa2_ref[...]) + ba2_ref[...]
        return qk, gv, gate, sq, sk, sv, y, head_mean_squares(sq), head_mean_squares(sk)

    def store_cache_layout(ref, row0, a):
        rows = ref.reshape(x_ref.shape[0] * SB_HEADS, SB_HD)
        for h in range(SB_HEADS):
            rows[pl.ds(row0 * SB_HEADS + h, a.shape[0], stride=SB_HEADS), :] = a[:, h * SB_HD:(h + 1) * SB_HD]

    def write(row0, qk, gv, gate, sq, sk, sv, y, ms_q, ms_k):
        rows = slice(row0, row0 + rows_per_split)
        gq_ref[rows, :] = qk[:, :GLA_KEY] * (GLA_DK ** -0.5)
        gk_ref[rows, :] = qk[:, GLA_KEY:]
        gv_ref[rows, :] = gv.astype(BF16)
        gate_ref[rows, :] = gate
        svb_ref[rows, :] = sv.astype(BF16)
        store_cache_layout(v4_ref, row0, sv)
        la_ref[rows, :] = (jnp.minimum(y, 0.0) - jnp.log(1.0 + jnp.exp(-jnp.abs(y)))) * (1.0 / GATE_NORM)
        sq_ref[rows, :] = (sq * lax.rsqrt(ms_q + EPS) * qg_ref[...] * (SB_HD ** -0.5)).astype(BF16)
        sk_n = sk * lax.rsqrt(ms_k + EPS) * kg_ref[...]
        skb_ref[rows, :] = sk_n.astype(BF16)
        store_cache_layout(k4_ref, row0, sk_n)

    row0s = [i * rows_per_split for i in range(n_split)]
    xns = [_rmsnorm_rows(x_ref[r0:r0 + rows_per_split, :], g1_ref[...]).astype(BF16) for r0 in row0s]
    results = [products(xn) for xn in xns]
    for r0, res in zip(row0s, results):
        write(r0, *res)


def _project(x2d, bsz, pw, layer, kv_bufs, tm):
    n = x2d.shape[0]
    t = n // bsz
    aliased = kv_bufs is not None
    row = lambda w: pl.BlockSpec((tm, w), lambda i: (i, 0))
    per_layer = lambda a: pl.BlockSpec((None,) + a.shape[1:], lambda i: (layer,) + (0,) * (a.ndim - 1))
    weights = (pw['g1'], pw['win'], pw['wsb'], pw['wa2'], pw['ba2'], pw['qg'], pw['kg'])
    weight_specs = [per_layer(w) for w in weights]
    weight_specs[1] = pl.BlockSpec((None, D_MODEL, GLA_COLS), lambda i: (layer, 0, 0))
    out_cols = ((GLA_KEY, F32), (GLA_KEY, F32), (GLA_KEY, F32), (GLA_WIDTH, BF16), (GLA_WIDTH, F32),
                (SB_WIDTH, BF16), (SB_WIDTH, BF16), (SB_WIDTH, BF16))
    if tm <= t:
        kv_spec = pl.BlockSpec((None, 1, tm, SB_HEADS, SB_HD), lambda i: (layer, i // (t // tm), i % (t // tm), 0, 0))
    else:
        kv_spec = pl.BlockSpec((None, tm // t, t, SB_HEADS, SB_HD), lambda i: (layer, i, 0, 0, 0))
    kv_shape = jax.ShapeDtypeStruct((DEPTH, bsz, t, SB_HEADS, SB_HD), F32)
    n_in = 1 + len(weights)
    return pl.pallas_call(
        functools.partial(_project_kernel, aliased=aliased, n_split=1),
        grid=(n // tm,),
        in_specs=[row(D_MODEL)] + weight_specs + ([pl.BlockSpec(memory_space=pl.ANY)] * 2 if aliased else []),
        out_specs=[row(w) for w, _ in out_cols] + [kv_spec, kv_spec],
        out_shape=[jax.ShapeDtypeStruct((n, w), dt) for w, dt in out_cols] + [kv_shape, kv_shape],
        input_output_aliases={n_in: len(out_cols), n_in + 1: len(out_cols) + 1} if aliased else {},
        compiler_params=pltpu.CompilerParams(dimension_semantics=("parallel",), vmem_limit_bytes=VMEM_LIMIT),
        name="project",
    )(x2d, *weights, *(kv_bufs if aliased else ()))


def _gla_kernel(*refs, n_chunks, chunks_per_step, has_init):
    if has_init:
        q_ref, k_ref, la_ref, v_ref, s0_ref, o_ref, sout_ref, s_scr, b_scr = refs
    else:
        q_ref, k_ref, la_ref, v_ref, o_ref, sout_ref, s_scr, b_scr = refs
    g = pl.program_id(2)

    @pl.when(g == 0)
    def _():
        if has_init:
            s_scr[...] = s0_ref[...].reshape(2 * GLA_DK, GLA_DV)
        else:
            s_scr[...] = jnp.zeros_like(s_scr)

    rr = _iota((CHUNK, CHUNK), 0)
    cc = _iota((CHUNK, CHUNK), 1)
    tri = jnp.where(cc <= rr, 1.0, 0.0).astype(BF16)
    r = _iota((2 * LANES, 2 * LANES), 0) // GLA_DK
    c = _iota((2 * LANES, 2 * LANES), 1) // GLA_DK
    ones_bd2 = jnp.where(r == c, 1.0, 0.0).astype(BF16)
    lane = _iota((CHUNK, LANES), 1)
    lane_t = lane % CHUNK
    tile_t = _iota((ROW_TILE, LANES), 1) % CHUNK
    srow = _iota((CHUNK, LANES), 0)
    head_a = lane < GLA_DK

    n_sub = CHUNK // ROW_TILE
    sub = lambda g: slice(g * ROW_TILE, (g + 1) * ROW_TILE)

    def operands(slot, base, b):
        b2 = b * LOG2E
        b_scr[slot] = b2
        rows = pl.ds(base, CHUNK)
        q = q_ref[rows, :]
        k = k_ref[rows, :]

        b_end = jnp.concatenate(
            [jnp.broadcast_to(b[(g + 1) * ROW_TILE - 1:(g + 1) * ROW_TILE], (ROW_TILE, LANES)) for g in range(n_sub)],
            axis=0)
        k_dec = (k * jnp.exp(b_end - b)).astype(BF16)
        q_dec_rhs = []
        for g in range(n_sub - 1):
            lo = (g + 1) * ROW_TILE
            q_dec = q[lo:] * jnp.exp(b[lo:] - b[lo - 1:lo])
            pad = jnp.zeros((lo, LANES), F32)
            in_a = _iota((CHUNK - lo, LANES), 1) < GLA_DK
            q_dec_rhs.append(
                jnp.concatenate([pad, jnp.where(in_a, q_dec, 0.0), pad, jnp.where(in_a, 0.0, q_dec)],
                                axis=0).astype(BF16))

        es = []
        for t in range(CHUNK):
            g = t // ROW_TILE
            brow = b_scr[slot, t:t + 1, :]
            qrow = q_ref[pl.ds(base + t, 1), :]
            es.append(jnp.exp2(brow - b2[sub(g)]).astype(BF16) * (qrow * k[sub(g)]).astype(BF16))
        diag_lhs = jnp.concatenate(
            [jnp.concatenate([es[2 * p], es[2 * p + 1]], axis=1) for p in range(CHUNK // 2)], axis=0)

        qe = q * jnp.exp(b)
        qe_a = jnp.where(head_a, qe, 0.0).astype(BF16)
        qe_b = jnp.where(head_a, 0.0, qe).astype(BF16)
        b_t = b.T
        b_last = b_t[:, CHUNK - 1:CHUNK]
        kd_t = (k.T * jnp.exp(b_last - b_t)).astype(BF16)
        return k_dec, q_dec_rhs, diag_lhs, qe_a, qe_b, kd_t, jnp.exp(b_last)

    def scores(off_tiles, red):
        pt_tiles = list(off_tiles) + [jnp.zeros((ROW_TILE, LANES), F32)]
        for t in range(CHUNK):
            g, p, half = t // ROW_TILE, t // 2, t % 2
            r_t = red[p * ROW_TILE:(p + 1) * ROW_TILE, half * LANES:(half + 1) * LANES]
            pt_tiles[g] = jnp.where(tile_t == t, r_t, pt_tiles[g])
        pt = jnp.concatenate(pt_tiles, axis=0)
        return jnp.where(srow <= lane_t, pt, 0.0)

    def step_body(si, carry):
        slots = range(chunks_per_step)
        bases = [pl.multiple_of((si * chunks_per_step + slot) * CHUNK, CHUNK) for slot in slots]
        bs = []
        for base in bases:
            p1, p2, p3 = _split3(la_ref[pl.ds(base, CHUNK), :])
            bs.append(_dot(tri, p1) + _dot(tri, p2) + _dot(tri, p3))
        ops = [operands(slot, bases[slot], bs[slot]) for slot in slots]
        v2s = [v_ref[pl.ds(base, CHUNK), :] for base in bases]
        offs = [[_dot_nt(op[0][sub(g)], op[1][g]) for g in range(n_sub - 1)] for op in ops]
        reds = [_dot(op[2], ones_bd2) for op in ops]
        upds = [_dot(op[5], v2) for op, v2 in zip(ops, v2s)]
        pts = [scores(off, red).T.astype(BF16) for off, red in zip(offs, reds)]
        o_intras = [_dot(pt, v2) for pt, v2 in zip(pts, v2s)]

        s = s_scr[...]
        for slot in slots:
            _, _, _, qe_a, qe_b, _, decay = ops[slot]
            rows = pl.ds(bases[slot], CHUNK)
            s_bf = s.astype(BF16)
            o_ref[rows, :GLA_DV] = o_intras[slot][:CHUNK, :GLA_DV] + _dot(qe_a, s_bf)
            o_ref[rows, GLA_DV:] = o_intras[slot][CHUNK:, GLA_DV:] + _dot(qe_b, s_bf)
            upd = upds[slot]
            s = decay * s + jnp.concatenate([upd[:GLA_DK, :GLA_DV], upd[GLA_DK:, GLA_DV:]], axis=0)
        s_scr[...] = s
        return carry

    lax.fori_loop(0, n_chunks // chunks_per_step, step_body, 0)

    @pl.when(g == pl.num_programs(2) - 1)
    def _():
        sout_ref[...] = s_scr[...].reshape(2, GLA_DK, GLA_DV)


def _gla(gq, gk, la, gv, s0, layer, tg):
    bsz, t, _ = gq.shape
    has_init = s0 is not None
    n_chunks = tg // CHUNK
    chunks_per_step = 16 if n_chunks % 16 == 0 else 1
    qspec = pl.BlockSpec((None, tg, LANES), lambda b, p, g: (b, g, p))
    vspec = pl.BlockSpec((None, tg, 2 * GLA_DV), lambda b, p, g: (b, g, p))
    sspec = pl.BlockSpec((None, 2, GLA_DK, GLA_DV), lambda b, p, g: (b, p, 0, 0))
    s0spec = pl.BlockSpec((None, None, 2, GLA_DK, GLA_DV), lambda b, p, g: (layer, b, p, 0, 0))
    in_specs = [qspec, qspec, qspec, vspec] + ([s0spec] if has_init else [])
    args = (gq, gk, la, gv) + ((s0,) if has_init else ())
    return pl.pallas_call(
        functools.partial(_gla_kernel, n_chunks=n_chunks, chunks_per_step=chunks_per_step, has_init=has_init),
        grid=(bsz, GLA_HEADS // 2, t // tg),
        in_specs=in_specs,
        out_specs=[vspec, sspec],
        out_shape=[jax.ShapeDtypeStruct((bsz, t, GLA_WIDTH), F32),
                   jax.ShapeDtypeStruct((bsz, GLA_HEADS, GLA_DK, GLA_DV), F32)],
        scratch_shapes=[pltpu.VMEM((2 * GLA_DK, GLA_DV), F32), pltpu.VMEM((chunks_per_step, CHUNK, LANES), F32)],
        compiler_params=pltpu.CompilerParams(
            dimension_semantics=("parallel", "parallel", "arbitrary"), vmem_limit_bytes=VMEM_LIMIT),
        name="gla",
    )(*args)


def _cumsum_matrix(tk):
    r = _iota((tk, 2 * tk), 0)
    c = _iota((tk, 2 * tk), 1)
    return jnp.where((c >= tk) | (r > c), 1.0, 0.0).astype(BF16)


def _sb_unit(streams):
    n_keys = lambda b: b[0].shape[1] if b[4] else b[0].shape[0]
    zs = [[_dot(q2, b[0]) if b[4] else _dot_nt(q2, b[0]) for b in blocks] for q2, blocks, _, _ in streams]
    sps = []
    for z_list, (_, blocks, _, _) in zip(zs, streams):
        sp_list = []
        for z, (_, _, _, mask, _) in zip(z_list, blocks):
            sp = _softplus(z)
            sp_list.append(sp if mask is None else jnp.where(mask, sp, 0.0))
        sps.append(sp_list)
    crs = []
    for sp_list, (q2, blocks, _, _) in zip(sps, streams):
        cr_list = [None] * len(blocks)
        rows = q2.shape[0]
        for tk in sorted({n_keys(b) for b in blocks}):
            idx = [n for n, b in enumerate(blocks) if n_keys(b) == tk]
            u = blocks[idx[0]][2]
            cr = _dot(jnp.concatenate([sp_list[n] for n in idx], axis=0).astype(BF16), u)
            for m, n in enumerate(idx):
                cr_list[n] = cr[m * rows:(m + 1) * rows]
        crs.append(cr_list)
    ws = []
    carries = []
    for z_list, sp_list, cr_list, (_, blocks, carry, _) in zip(zs, sps, crs, streams):
        w_list = []
        for z, sp, cr, block in zip(z_list, sp_list, cr_list, blocks):
            tk, mask = n_keys(block), block[3]
            w = jnp.exp(z - sp - cr[:, :tk] - carry[:, :tk])
            if mask is not None:
                w = jnp.where(mask, w, 0.0)
            w_list.append(w.astype(BF16))
            rs = cr[:, tk:]
            if tk < LANES:
                rs = jnp.concatenate([rs] * (LANES // tk), axis=1)
            carry = carry + rs
        ws.append(w_list)
        carries.append(carry)
    out = []
    for w_list, carry, (_, blocks, _, acc) in zip(ws, carries, streams):
        pv = None
        for w, (_, vj, _, _, transposed) in zip(w_list, blocks):
            d = _dot_nt(w, vj) if transposed else _dot(w, vj)
            pv = d if pv is None else pv + d
        out.append((carry, acc + pv))
    return out


def _sb_kernel(*refs, tq, n_sub, n_past_blocks):
    if n_past_blocks:
        q_ref, k_ref, v_ref, pk_ref, pv_ref, o_ref = refs
    else:
        q_ref, k_ref, v_ref, o_ref = refs
    i = pl.program_id(2)
    lane = _iota((tq, LANES), 1)
    u_own = _cumsum_matrix(tq)
    strict = _iota((2 * tq, tq), 1) < _iota((2 * tq, tq), 0) % tq
    zeros = jnp.zeros((2 * tq, LANES), F32)

    def stacked_heads(r):
        q = q_ref[r * tq:(r + 1) * tq, :]
        zero_q = jnp.zeros_like(q)
        return jnp.concatenate([jnp.where(lane < SB_HD, q, zero_q), jnp.where(lane < SB_HD, zero_q, q)], axis=0)

    q2s = [stacked_heads(r) for r in range(n_sub)]

    if n_past_blocks:
        u_past = _cumsum_matrix(LANES)

        def past_block(ref, j):
            return ref[:, :, j * LANES:(j + 1) * LANES].reshape(2 * SB_HD, LANES).astype(BF16)

        blocks = [(k_ref[...], v_ref[...], u_own, strict, False)]
        for j in reversed(range(n_past_blocks)):
            blocks.append((past_block(pk_ref, j), past_block(pv_ref, j), u_past, None, True))
        accs = [_sb_unit([(q2s[0], blocks, zeros, zeros)])[0][1]]
    else:
        def load_unit(r, n, first):
            blocks = []
            for kk in range(SB_FIRST_UNIT if first else SB_NEXT_UNIT):
                j = i * n_sub + r - kk - (0 if first else SB_FIRST_UNIT + (n - 1) * SB_NEXT_UNIT)
                rows = pl.ds(pl.multiple_of(jnp.maximum(j, 0) * tq, tq), tq)
                vj = v_ref[rows, :]
                diag = first and kk == 0
                if not diag:
                    vj = jnp.where(j >= 0, vj, jnp.zeros_like(vj))
                blocks.append((k_ref[rows, :], vj, u_own, strict if diag else None, False))
            return blocks

        def min_carry(state):
            m = state[0][0]
            for carry, _ in state[1:]:
                m = jnp.minimum(m, carry)
            return jnp.min(m)

        state = _sb_unit([(q2s[r], load_unit(r, 0, True), zeros, zeros) for r in range(n_sub)])
        blocks_left = jnp.maximum(i * n_sub + n_sub - SB_FIRST_UNIT, 0)
        n_units = 1 + (blocks_left + SB_NEXT_UNIT - 1) // SB_NEXT_UNIT

        def cond(st):
            return (st[0] < n_units) & (st[1] < SB_ZERO_LOG)

        def body(st):
            n, _, state = st
            state = _sb_unit([(q2s[r], load_unit(r, n, False), *state[r]) for r in range(n_sub)])
            return n + 1, min_carry(state), state

        state = lax.while_loop(cond, body, (jnp.int32(1), min_carry(state), state))[2]
        accs = [acc for _, acc in state]
    for r, acc in enumerate(accs):
        o_ref[r * tq:(r + 1) * tq, :] = jnp.where(lane < SB_HD, acc[:tq], acc[tq:]).astype(o_ref.dtype)


def _sb(q, k, v, past_k, past_v, layer, tq, n_sub):
    bsz, t, _ = q.shape
    n_past_blocks = 0 if past_k is None else past_k.shape[-1] // LANES
    qspec = pl.BlockSpec((None, tq * n_sub, LANES), lambda b, p, i: (b, i, p))
    kspec = pl.BlockSpec((None, t, LANES), lambda b, p, i: (b, 0, p))
    in_specs = [qspec, kspec, kspec]
    args = (q, k, v)
    if n_past_blocks:
        pspec = pl.BlockSpec((None, None, 2, SB_HD, past_k.shape[-1]), lambda b, p, i: (layer, b, p, 0, 0))
        in_specs += [pspec, pspec]
        args += (past_k, past_v)
    return pl.pallas_call(
        functools.partial(_sb_kernel, tq=tq, n_sub=n_sub, n_past_blocks=n_past_blocks),
        grid=(bsz, SB_WIDTH // LANES, t // (tq * n_sub)),
        in_specs=in_specs,
        out_specs=qspec,
        out_shape=jax.ShapeDtypeStruct((bsz, t, SB_WIDTH), BF16),
        compiler_params=pltpu.CompilerParams(
            dimension_semantics=("parallel", "parallel", "arbitrary"), vmem_limit_bytes=VMEM_LIMIT),
        name="sb",
    )(*args)


def _finish_kernel(x_ref, og_ref, gate_ref, osb_ref, gg_ref, wo_ref, g2_ref, wup_ref, wdn_ref, y_ref, *, ff_block):
    og = og_ref[...]
    normed = [_rmsnorm_rows(og[:, h * GLA_DV:(h + 1) * GLA_DV], gg_ref[...]) for h in range(GLA_HEADS)]
    gate = gate_ref[...]
    mix_g = (jnp.concatenate(normed, axis=-1) * (gate * jax.nn.sigmoid(gate))).astype(BF16)
    h = x_ref[...] + _dot(mix_g, wo_ref[:GLA_WIDTH, :]) + _dot(osb_ref[...], wo_ref[GLA_WIDTH:, :])
    hn = _rmsnorm_rows(h, g2_ref[...]).astype(BF16)
    mlp = None
    for j in range(D_FF // ff_block):
        cols = slice(j * ff_block, (j + 1) * ff_block)
        u = _dot(hn, wup_ref[:, cols])
        a = jnp.square(jnp.maximum(u, 0.0)).astype(BF16)
        d = _dot(a, wdn_ref[cols, :])
        mlp = d if mlp is None else mlp + d
    y_ref[...] = h + mlp


def _finish(x2d, og, gate, osb, pw, layer, tm):
    n = x2d.shape[0]
    row = lambda w: pl.BlockSpec((tm, w), lambda i: (i, 0))
    per_layer = lambda a: pl.BlockSpec((None,) + a.shape[1:], lambda i: (layer,) + (0,) * (a.ndim - 1),
                                       pipeline_mode=pl.Buffered(1))
    weights = (pw['gg'], pw['wo'], pw['g2'], pw['wup'], pw['wdn'])
    return pl.pallas_call(
        functools.partial(_finish_kernel, ff_block=1024),
        grid=(n // tm,),
        in_specs=[row(D_MODEL), row(GLA_WIDTH), row(GLA_WIDTH), row(SB_WIDTH)] + [per_layer(w) for w in weights],
        out_specs=row(D_MODEL),
        out_shape=jax.ShapeDtypeStruct((n, D_MODEL), F32),
        compiler_params=pltpu.CompilerParams(dimension_semantics=("parallel",), vmem_limit_bytes=VMEM_LIMIT),
        name="finish",
    )(x2d, og, gate, osb, *weights)


def _prepare_weights(norm1_g, w_in, w_a2, b_a2, q_norm_g, k_norm_g, gla_norm_g, w_out, norm2_g, w_up, w_down):
    w_in = w_in.astype(BF16)
    row = lambda a: a.reshape(DEPTH, 1, a.shape[-1])
    return {
        'g1': row(norm1_g),
        'win': w_in,
        'wsb': w_in[:, :, SB_COL0:],
        'wa2': jnp.pad(w_a2, ((0, 0), (0, LANES - GATE_RANK), (0, 0))).astype(BF16),
        'ba2': row(b_a2),
        'qg': row(jnp.tile(q_norm_g, (1, SB_HEADS))),
        'kg': row(jnp.tile(k_norm_g, (1, SB_HEADS))),
        'gg': row(gla_norm_g),
        'wo': w_out.astype(BF16),
        'g2': row(norm2_g),
        'wup': w_up.astype(BF16),
        'wdn': w_down.astype(BF16),
    }


def _stream_layer(x, pw, layer, kv_bufs, past_k, past_v, s0, tm, tf, tg, tq, n_sub):
    bsz, t, _ = x.shape
    x2d = x.reshape(bsz * t, D_MODEL)
    gq, gk, la, gv, gate, sq, skb, svb, skf, svf = _project(x2d, bsz, pw, layer, kv_bufs, tm)
    r3 = lambda a: a.reshape(bsz, t, a.shape[-1])
    o_gla, s_new = _gla(r3(gq), r3(gk), r3(la), r3(gv), s0, layer, tg)
    o_sb = _sb(r3(sq), r3(skb), r3(svb), past_k, past_v, layer, tq, n_sub)
    y = _finish(x2d, o_gla.reshape(bsz * t, GLA_WIDTH), gate, o_sb.reshape(bsz * t, SB_WIDTH), pw, layer, tf)
    return y.reshape(bsz, t, D_MODEL), (skf, svf), s_new


def kernel(x_prompt, x_sample, cache_sb_k, cache_sb_v, state_gla, norm1_g, w_in, w_a2, b_a2, q_norm_g, k_norm_g,
           gla_norm_g, w_out, norm2_g, w_up, w_down):
    pw = _prepare_weights(norm1_g, w_in, w_a2, b_a2, q_norm_g, k_norm_g, gla_norm_g, w_out, norm2_g, w_up, w_down)
    past_k = jnp.transpose(cache_sb_k, (0, 1, 3, 4, 2))
    past_v = jnp.transpose(cache_sb_v, (0, 1, 3, 4, 2))
    xp, xs = x_prompt, x_sample
    kv_p, kv_s, states_p, states_s = None, None, [], []
    for layer in range(DEPTH):
        xp, kv_p, sp = _stream_layer(xp, pw, layer, kv_p, None, None, None, tm=512, tf=512, tg=1024, tq=128, n_sub=8)
        xs, kv_s, ss = _stream_layer(xs, pw, layer, kv_s, past_k, past_v, state_gla, tm=512, tf=256, tg=CHUNK,
                                     tq=CHUNK, n_sub=1)
        states_p.append(sp)
        states_s.append(ss)
    return (xp, xs, kv_p[0], kv_p[1], jnp.stack(states_p), kv_s[0], kv_s[1], jnp.stack(states_s))
```

```python
import functools

import jax
import jax.numpy as jnp
from jax import lax
from jax.experimental import pallas as pl
from jax.experimental.pallas import tpu as pltpu

D_MODEL = 1024
DEPTH = 2
CHUNK = 64
GLA_HEADS = 4
GLA_DK = 64
GLA_DV = 128
GLA_KEY = GLA_HEADS * GLA_DK
GLA_WIDTH = GLA_HEADS * GLA_DV
GATE_RANK = 16
GATE_NORM = 16.0
SB_HEADS = 8
SB_HD = 64
SB_WIDTH = SB_HEADS * SB_HD
D_FF = 4 * D_MODEL
EPS = 1e-6

LANES = 128
ROW_TILE = 16
SB_FIRST_UNIT = 3
SB_NEXT_UNIT = 1
SB_ZERO_LOG = 110.0
LOG2E = 1.4426950408889634
VMEM_LIMIT = 56 * 1024 * 1024

_GATE_COLS = 2 * GLA_KEY + 2 * GLA_WIDTH
_COL_QK = (0, 2 * GLA_KEY)
_COL_GV = (2 * GLA_KEY, 2 * GLA_KEY + GLA_WIDTH)
_COL_GATE = (2 * GLA_KEY + GLA_WIDTH, _GATE_COLS)
_COL_ALR = (_GATE_COLS, _GATE_COLS + LANES)
GLA_COLS = _COL_ALR[1]
SB_COL0 = _GATE_COLS + GATE_RANK
_COL_SQ = (0, SB_WIDTH)
_COL_SK = (SB_WIDTH, 2 * SB_WIDTH)
_COL_SV = (2 * SB_WIDTH, 3 * SB_WIDTH)

F32 = jnp.float32
BF16 = jnp.bfloat16


def _dot(a, b):
    return jnp.dot(a, b, preferred_element_type=F32)


def _dot_nt(a, b):
    return lax.dot_general(a, b, (((1,), (1,)), ((), ())), preferred_element_type=F32)


def _iota(shape, dim):
    return lax.broadcasted_iota(jnp.int32, shape, dim)


def _split3(x):
    p1 = x.astype(BF16)
    r1 = x - p1.astype(F32)
    p2 = r1.astype(BF16)
    p3 = (r1 - p2.astype(F32)).astype(BF16)
    return p1, p2, p3


def _softplus(z):
    return jnp.maximum(z, 0.0) + jnp.log(1.0 + jnp.exp(-jnp.abs(z)))


def _rmsnorm_rows(x, g):
    ms = jnp.mean(x * x, axis=-1, keepdims=True)
    return x * lax.rsqrt(ms + EPS) * g


def _project_kernel(*refs, aliased, n_split):
    x_ref, g1_ref, wgla_ref, wsb_ref, wa2_ref, ba2_ref, qg_ref, kg_ref = refs[:8]
    gq_ref, gk_ref, la_ref, gv_ref, gate_ref, sq_ref, skb_ref, svb_ref, k4_ref, v4_ref = refs[8 + 2 * aliased:]
    rows_per_split = x_ref.shape[0] // n_split

    r = _iota((LANES, LANES), 0) // SB_HD
    c = _iota((LANES, LANES), 1) // SB_HD
    ones_bd = jnp.where(r == c, 1.0, 0.0).astype(BF16)

    def head_mean_squares(s):
        out = []
        for j in range(SB_WIDTH // LANES):
            sj = s[:, j * LANES:(j + 1) * LANES]
            out.append(_dot((sj * sj).astype(BF16), ones_bd) * (1.0 / SB_HD))
        return jnp.concatenate(out, axis=-1)

    def products(xn):
        proj = lambda w_ref, cols: _dot(xn, w_ref[:, cols[0]:cols[1]])
        qk = proj(wgla_ref, _COL_QK)
        gv = proj(wgla_ref, _COL_GV)
        gate = proj(wgla_ref, _COL_GATE)
        alr = proj(wgla_ref, _COL_ALR)
        sq = proj(wsb_ref, _COL_SQ)
        sk = proj(wsb_ref, _COL_SK)
        sv = proj(wsb_ref, _COL_SV)
        y = _dot(alr.astype(BF16), wa2_ref[...]) + ba2_ref[...]
        return qk, gv, gate, sq, sk, sv, y, head_mean_squares(sq), head_mean_squares(sk)

    def store_cache_layout(ref, row0, a):
        rows = ref.reshape(x_ref.shape[0] * SB_HEADS, SB_HD)
        for h in range(SB_HEADS):
            rows[pl.ds(row0 * SB_HEADS + h, a.shape[0], stride=SB_HEADS), :] = a[:, h * SB_HD:(h + 1) * SB_HD]

    def write(row0, qk, gv, gate, sq, sk, sv, y, ms_q, ms_k):
        rows = slice(row0, row0 + rows_per_split)
        gq_ref[rows, :] = qk[:, :GLA_KEY] * (GLA_DK ** -0.5)
        gk_ref[rows, :] = qk[:, GLA_KEY:]
        gv_ref[rows, :] = gv.astype(BF16)
        gate_ref[rows, :] = gate
        svb_ref[rows, :] = sv.astype(BF16)
        store_cache_layout(v4_ref, row0, sv)
        la_ref[rows, :] = (jnp.minimum(y, 0.0) - jnp.log(1.0 + jnp.exp(-jnp.abs(y)))) * (1.0 / GATE_NORM)
        sq_ref[rows, :] = (sq * lax.rsqrt(ms_q + EPS) * qg_ref[...] * (SB_HD ** -0.5)).astype(BF16)
        sk_n = sk * lax.rsqrt(ms_k + EPS) * kg_ref[...]
        skb_ref[rows, :] = sk_n.astype(BF16)
        store_cache_layout(k4_ref, row0, sk_n)

    row0s = [i * rows_per_split for i in range(n_split)]
    xns = [_rmsnorm_rows(x_ref[r0:r0 + rows_per_split, :], g1_ref[...]).astype(BF16) for r0 in row0s]
    results = [products(xn) for xn in xns]
    for r0, res in zip(row0s, results):
        write(r0, *res)


def _project(x2d, bsz, pw, layer, kv_bufs, tm):
    n = x2d.shape[0]
    t = n // bsz
    aliased = kv_bufs is not None
    row = lambda w: pl.BlockSpec((tm, w), lambda i: (i, 0))
    per_layer = lambda a: pl.BlockSpec((None,) + a.shape[1:], lambda i: (layer,) + (0,) * (a.ndim - 1))
    weights = (pw['g1'], pw['win'], pw['wsb'], pw['wa2'], pw['ba2'], pw['qg'], pw['kg'])
    weight_specs = [per_layer(w) for w in weights]
    weight_specs[1] = pl.BlockSpec((None, D_MODEL, GLA_COLS), lambda i: (layer, 0, 0))
    out_cols = ((GLA_KEY, F32), (GLA_KEY, F32), (GLA_KEY, F32), (GLA_WIDTH, BF16), (GLA_WIDTH, F32),
                (SB_WIDTH, BF16), (SB_WIDTH, BF16), (SB_WIDTH, BF16))
    if tm <= t:
        kv_spec = pl.BlockSpec((None, 1, tm, SB_HEADS, SB_HD), lambda i: (layer, i // (t // tm), i % (t // tm), 0, 0))
    else:
        kv_spec = pl.BlockSpec((None, tm // t, t, SB_HEADS, SB_HD), lambda i: (layer, i, 0, 0, 0))
    kv_shape = jax.ShapeDtypeStruct((DEPTH, bsz, t, SB_HEADS, SB_HD), F32)
    n_in = 1 + len(weights)
    return pl.pallas_call(
        functools.partial(_project_kernel, aliased=aliased, n_split=1),
        grid=(n // tm,),
        in_specs=[row(D_MODEL)] + weight_specs + ([pl.BlockSpec(memory_space=pl.ANY)] * 2 if aliased else []),
        out_specs=[row(w) for w, _ in out_cols] + [kv_spec, kv_spec],
        out_shape=[jax.ShapeDtypeStruct((n, w), dt) for w, dt in out_cols] + [kv_shape, kv_shape],
        input_output_aliases={n_in: len(out_cols), n_in + 1: len(out_cols) + 1} if aliased else {},
        compiler_params=pltpu.CompilerParams(dimension_semantics=("parallel",), vmem_limit_bytes=VMEM_LIMIT),
        name="project",
    )(x2d, *weights, *(kv_bufs if aliased else ()))


def _gla_kernel(*refs, n_chunks, chunks_per_step, has_init):
    if has_init:
        q_ref, k_ref, la_ref, v_ref, s0_ref, o_ref, sout_ref, s_scr, b_scr = refs
    else:
        q_ref, k_ref, la_ref, v_ref, o_ref, sout_ref, s_scr, b_scr = refs
    g = pl.program_id(2)

    @pl.when(g == 0)
    def _():
        if has_init:
            s_scr[...] = s0_ref[...].reshape(2 * GLA_DK, GLA_DV)
        else:
            s_scr[...] = jnp.zeros_like(s_scr)

    rr = _iota((CHUNK, CHUNK), 0)
    cc = _iota((CHUNK, CHUNK), 1)
    tri = jnp.where(cc <= rr, 1.0, 0.0).astype(BF16)
    r = _iota((2 * LANES, 2 * LANES), 0) // GLA_DK
    c = _iota((2 * LANES, 2 * LANES), 1) // GLA_DK
    ones_bd2 = jnp.where(r == c, 1.0, 0.0).astype(BF16)
    lane = _iota((CHUNK, LANES), 1)
    lane_t = lane % CHUNK
    tile_t = _iota((ROW_TILE, LANES), 1) % CHUNK
    srow = _iota((CHUNK, LANES), 0)
    head_a = lane < GLA_DK

    n_sub = CHUNK // ROW_TILE
    sub = lambda g: slice(g * ROW_TILE, (g + 1) * ROW_TILE)

    def operands(slot, base, b):
        b2 = b * LOG2E
        b_scr[slot] = b2
        rows = pl.ds(base, CHUNK)
        q = q_ref[rows, :]
        k = k_ref[rows, :]

        b_end = jnp.concatenate(
            [jnp.broadcast_to(b[(g + 1) * ROW_TILE - 1:(g + 1) * ROW_TILE], (ROW_TILE, LANES)) for g in range(n_sub)],
            axis=0)
        k_dec = (k * jnp.exp(b_end - b)).astype(BF16)
        q_dec_rhs = []
        for g in range(n_sub - 1):
            lo = (g + 1) * ROW_TILE
            q_dec = q[lo:] * jnp.exp(b[lo:] - b[lo - 1:lo])
            pad = jnp.zeros((lo, LANES), F32)
            in_a = _iota((CHUNK - lo, LANES), 1) < GLA_DK
            q_dec_rhs.append(
                jnp.concatenate([pad, jnp.where(in_a, q_dec, 0.0), pad, jnp.where(in_a, 0.0, q_dec)],
                                axis=0).astype(BF16))

        es = []
        for t in range(CHUNK):
            g = t // ROW_TILE
            brow = b_scr[slot, t:t + 1, :]
            qrow = q_ref[pl.ds(base + t, 1), :]
            es.append(jnp.exp2(brow - b2[sub(g)]).astype(BF16) * (qrow * k[sub(g)]).astype(BF16))
        diag_lhs = jnp.concatenate(
            [jnp.concatenate([es[2 * p], es[2 * p + 1]], axis=1) for p in range(CHUNK // 2)], axis=0)

        qe = q * jnp.exp(b)
        qe_a = jnp.where(head_a, qe, 0.0).astype(BF16)
        qe_b = jnp.where(head_a, 0.0, qe).astype(BF16)
        b_t = b.T
        b_last = b_t[:, CHUNK - 1:CHUNK]
        kd_t = (k.T * jnp.exp(b_last - b_t)).astype(BF16)
        return k_dec, q_dec_rhs, diag_lhs, qe_a, qe_b, kd_t, jnp.exp(b_last)

    def scores(off_tiles, red):
        pt_tiles = list(off_tiles) + [jnp.zeros((ROW_TILE, LANES), F32)]
        for t in range(CHUNK):
            g, p, half = t // ROW_TILE, t // 2, t % 2
            r_t = red[p * ROW_TILE:(p + 1) * ROW_TILE, half * LANES:(half + 1) * LANES]
            pt_tiles[g] = jnp.where(tile_t == t, r_t, pt_tiles[g])
        pt = jnp.concatenate(pt_tiles, axis=0)
        return jnp.where(srow <= lane_t, pt, 0.0)

    def step_body(si, carry):
        slots = range(chunks_per_step)
        bases = [pl.multiple_of((si * chunks_per_step + slot) * CHUNK, CHUNK) for slot in slots]
        bs = []
        for base in bases:
            p1, p2, p3 = _split3(la_ref[pl.ds(base, CHUNK), :])
            bs.append(_dot(tri, p1) + _dot(tri, p2) + _dot(tri, p3))
        ops = [operands(slot, bases[slot], bs[slot]) for slot in slots]
        v2s = [v_ref[pl.ds(base, CHUNK), :] for base in bases]
        offs = [[_dot_nt(op[0][sub(g)], op[1][g]) for g in range(n_sub - 1)] for op in ops]
        reds = [_dot(op[2], ones_bd2) for op in ops]
        upds = [_dot(op[5], v2) for op, v2 in zip(ops, v2s)]
        pts = [scores(off, red).T.astype(BF16) for off, red in zip(offs, reds)]
        o_intras = [_dot(pt, v2) for pt, v2 in zip(pts, v2s)]

        s = s_scr[...]
        for slot in slots:
            _, _, _, qe_a, qe_b, _, decay = ops[slot]
            rows = pl.ds(bases[slot], CHUNK)
            s_bf = s.astype(BF16)
            o_ref[rows, :GLA_DV] = o_intras[slot][:CHUNK, :GLA_DV] + _dot(qe_a, s_bf)
            o_ref[rows, GLA_DV:] = o_intras[slot][CHUNK:, GLA_DV:] + _dot(qe_b, s_bf)
            upd = upds[slot]
            s = decay * s + jnp.concatenate([upd[:GLA_DK, :GLA_DV], upd[GLA_DK:, GLA_DV:]], axis=0)
        s_scr[...] = s
        return carry

    lax.fori_loop(0, n_chunks // chunks_per_step, step_body, 0)

    @pl.when(g == pl.num_programs(2) - 1)
    def _():
        sout_ref[...] = s_scr[...].reshape(2, GLA_DK, GLA_DV)


def _gla(gq, gk, la, gv, s0, layer, tg):
    bsz, t, _ = gq.shape
    has_init = s0 is not None
    n_chunks = tg // CHUNK
    chunks_per_step = 16 if n_chunks % 16 == 0 else 1
    qspec = pl.BlockSpec((None, tg, LANES), lambda b, p, g: (b, g, p))
    vspec = pl.BlockSpec((None, tg, 2 * GLA_DV), lambda b, p, g: (b, g, p))
    sspec = pl.BlockSpec((None, 2, GLA_DK, GLA_DV), lambda b, p, g: (b, p, 0, 0))
    s0spec = pl.BlockSpec((None, None, 2, GLA_DK, GLA_DV), lambda b, p, g: (layer, b, p, 0, 0))
    in_specs = [qspec, qspec, qspec, vspec] + ([s0spec] if has_init else [])
    args = (gq, gk, la, gv) + ((s0,) if has_init else ())
    return pl.pallas_call(
        functools.partial(_gla_kernel, n_chunks=n_chunks, chunks_per_step=chunks_per_step, has_init=has_init),
        grid=(bsz, GLA_HEADS // 2, t // tg),
        in_specs=in_specs,
        out_specs=[vspec, sspec],
        out_shape=[jax.ShapeDtypeStruct((bsz, t, GLA_WIDTH), F32),
                   jax.ShapeDtypeStruct((bsz, GLA_HEADS, GLA_DK, GLA_DV), F32)],
        scratch_shapes=[pltpu.VMEM((2 * GLA_DK, GLA_DV), F32), pltpu.VMEM((chunks_per_step, CHUNK, LANES), F32)],
        compiler_params=pltpu.CompilerParams(
            dimension_semantics=("parallel", "parallel", "arbitrary"), vmem_limit_bytes=VMEM_LIMIT),
        name="gla",
    )(*args)


def _cumsum_matrix(tk):
    r = _iota((tk, 2 * tk), 0)
    c = _iota((tk, 2 * tk), 1)
    return jnp.where((c >= tk) | (r > c), 1.0, 0.0).astype(BF16)


def _sb_unit(streams):
    n_keys = lambda b: b[0].shape[1] if b[4] else b[0].shape[0]
    zs = [[_dot(q2, b[0]) if b[4] else _dot_nt(q2, b[0]) for b in blocks] for q2, blocks, _, _ in streams]
    sps = []
    for z_list, (_, blocks, _, _) in zip(zs, streams):
        sp_list = []
        for z, (_, _, _, mask, _) in zip(z_list, blocks):
            sp = _softplus(z)
            sp_list.append(sp if mask is None else jnp.where(mask, sp, 0.0))
        sps.append(sp_list)
    crs = []
    for sp_list, (q2, blocks, _, _) in zip(sps, streams):
        cr_list = [None] * len(blocks)
        rows = q2.shape[0]
        for tk in sorted({n_keys(b) for b in blocks}):
            idx = [n for n, b in enumerate(blocks) if n_keys(b) == tk]
            u = blocks[idx[0]][2]
            cr = _dot(jnp.concatenate([sp_list[n] for n in idx], axis=0).astype(BF16), u)
            for m, n in enumerate(idx):
                cr_list[n] = cr[m * rows:(m + 1) * rows]
        crs.append(cr_list)
    ws = []
    carries = []
    for z_list, sp_list, cr_list, (_, blocks, carry, _) in zip(zs, sps, crs, streams):
        w_list = []
        for z, sp, cr, block in zip(z_list, sp_list, cr_list, blocks):
            tk, mask = n_keys(block), block[3]
            w = jnp.exp(z - sp - cr[:, :tk] - carry[:, :tk])
            if mask is not None:
                w = jnp.where(mask, w, 0.0)
            w_list.append(w.astype(BF16))
            rs = cr[:, tk:]
            if tk < LANES:
                rs = jnp.concatenate([rs] * (LANES // tk), axis=1)
            carry = carry + rs
        ws.append(w_list)
        carries.append(carry)
    out = []
    for w_list, carry, (_, blocks, _, acc) in zip(ws, carries, streams):
        pv = None
        for w, (_, vj, _, _, transposed) in zip(w_list, blocks):
            d = _dot_nt(w, vj) if transposed else _dot(w, vj)
            pv = d if pv is None else pv + d
        out.append((carry, acc + pv))
    return out


def _sb_kernel(*refs, tq, n_sub, n_past_blocks):
    if n_past_blocks:
        q_ref, k_ref, v_ref, pk_ref, pv_ref, o_ref = refs
    else:
        q_ref, k_ref, v_ref, o_ref = refs
    i = pl.program_id(2)
    lane = _iota((tq, LANES), 1)
    u_own = _cumsum_matrix(tq)
    strict = _iota((2 * tq, tq), 1) < _iota((2 * tq, tq), 0) % tq
    zeros = jnp.zeros((2 * tq, LANES), F32)

    def stacked_heads(r):
        q = q_ref[r * tq:(r + 1) * tq, :]
        zero_q = jnp.zeros_like(q)
        return jnp.concatenate([jnp.where(lane < SB_HD, q, zero_q), jnp.where(lane < SB_HD, zero_q, q)], axis=0)

    q2s = [stacked_heads(r) for r in range(n_sub)]

    if n_past_blocks:
        u_past = _cumsum_matrix(LANES)

        def past_block(ref, j):
            return ref[:, :, j * LANES:(j + 1) * LANES].reshape(2 * SB_HD, LANES).astype(BF16)

        blocks = [(k_ref[...], v_ref[...], u_own, strict, False)]
        for j in reversed(range(n_past_blocks)):
            blocks.append((past_block(pk_ref, j), past_block(pv_ref, j), u_past, None, True))
        accs = [_sb_unit([(q2s[0], blocks, zeros, zeros)])[0][1]]
    else:
        def load_unit(r, n, first):
            blocks = []
            for kk in range(SB_FIRST_UNIT if first else SB_NEXT_UNIT):
                j = i * n_sub + r - kk - (0 if first else SB_FIRST_UNIT + (n - 1) * SB_NEXT_UNIT)
                rows = pl.ds(pl.multiple_of(jnp.maximum(j, 0) * tq, tq), tq)
                vj = v_ref[rows, :]
                diag = first and kk == 0
                if not diag:
                    vj = jnp.where(j >= 0, vj, jnp.zeros_like(vj))
                blocks.append((k_ref[rows, :], vj, u_own, strict if diag else None, False))
            return blocks

        def min_carry(state):
            m = state[0][0]
            for carry, _ in state[1:]:
                m = jnp.minimum(m, carry)
            return jnp.min(m)

        state = _sb_unit([(q2s[r], load_unit(r, 0, True), zeros, zeros) for r in range(n_sub)])
        blocks_left = jnp.maximum(i * n_sub + n_sub - SB_FIRST_UNIT, 0)
        n_units = 1 + (blocks_left + SB_NEXT_UNIT - 1) // SB_NEXT_UNIT

        def cond(st):
            return (st[0] < n_units) & (st[1] < SB_ZERO_LOG)

        def body(st):
            n, _, state = st
            state = _sb_unit([(q2s[r], load_unit(r, n, False), *state[r]) for r in range(n_sub)])
            return n + 1, min_carry(state), state

        state = lax.while_loop(cond, body, (jnp.int32(1), min_carry(state), state))[2]
        accs = [acc for _, acc in state]
    for r, acc in enumerate(accs):
        o_ref[r * tq:(r + 1) * tq, :] = jnp.where(lane < SB_HD, acc[:tq], acc[tq:]).astype(o_ref.dtype)


def _sb(q, k, v, past_k, past_v, layer, tq, n_sub):
    bsz, t, _ = q.shape
    n_past_blocks = 0 if past_k is None else past_k.shape[-1] // LANES
    qspec = pl.BlockSpec((None, tq * n_sub, LANES), lambda b, p, i: (b, i, p))
    kspec = pl.BlockSpec((None, t, LANES), lambda b, p, i: (b, 0, p))
    in_specs = [qspec, kspec, kspec]
    args = (q, k, v)
    if n_past_blocks:
        pspec = pl.BlockSpec((None, None, 2, SB_HD, past_k.shape[-1]), lambda b, p, i: (layer, b, p, 0, 0))
        in_specs += [pspec, pspec]
        args += (past_k, past_v)
    return pl.pallas_call(
        functools.partial(_sb_kernel, tq=tq, n_sub=n_sub, n_past_blocks=n_past_blocks),
        grid=(bsz, SB_WIDTH // LANES, t // (tq * n_sub)),
        in_specs=in_specs,
        out_specs=qspec,
        out_shape=jax.ShapeDtypeStruct((bsz, t, SB_WIDTH), BF16),
        compiler_params=pltpu.CompilerParams(
            dimension_semantics=("parallel", "parallel", "arbitrary"), vmem_limit_bytes=VMEM_LIMIT),
        name="sb",
    )(*args)


def _finish_kernel(x_ref, og_ref, gate_ref, osb_ref, gg_ref, wo_ref, g2_ref, wup_ref, wdn_ref, y_ref, *, ff_block):
    og = og_ref[...]
    normed = [_rmsnorm_rows(og[:, h * GLA_DV:(h + 1) * GLA_DV], gg_ref[...]) for h in range(GLA_HEADS)]
    gate = gate_ref[...]
    mix_g = (jnp.concatenate(normed, axis=-1) * (gate * jax.nn.sigmoid(gate))).astype(BF16)
    h = x_ref[...] + _dot(mix_g, wo_ref[:GLA_WIDTH, :]) + _dot(osb_ref[...], wo_ref[GLA_WIDTH:, :])
    hn = _rmsnorm_rows(h, g2_ref[...]).astype(BF16)
    mlp = None
    for j in range(D_FF // ff_block):
        cols = slice(j * ff_block, (j + 1) * ff_block)
        u = _dot(hn, wup_ref[:, cols])
        a = jnp.square(jnp.maximum(u, 0.0)).astype(BF16)
        d = _dot(a, wdn_ref[cols, :])
        mlp = d if mlp is None else mlp + d
    y_ref[...] = h + mlp


def _finish(x2d, og, gate, osb, pw, layer, tm):
    n = x2d.shape[0]
    row = lambda w: pl.BlockSpec((tm, w), lambda i: (i, 0))
    per_layer = lambda a: pl.BlockSpec((None,) + a.shape[1:], lambda i: (layer,) + (0,) * (a.ndim - 1),
                                       pipeline_mode=pl.Buffered(1))
    weights = (pw['gg'], pw['wo'], pw['g2'], pw['wup'], pw['wdn'])
    return pl.pallas_call(
        functools.partial(_finish_kernel, ff_block=1024),
        grid=(n // tm,),
        in_specs=[row(D_MODEL), row(GLA_WIDTH), row(GLA_WIDTH), row(SB_WIDTH)] + [per_layer(w) for w in weights],
        out_specs=row(D_MODEL),
        out_shape=jax.ShapeDtypeStruct((n, D_MODEL), F32),
        compiler_params=pltpu.CompilerParams(dimension_semantics=("parallel",), vmem_limit_bytes=VMEM_LIMIT),
        name="finish",
    )(x2d, og, gate, osb, *weights)


def _prepare_weights(norm1_g, w_in, w_a2, b_a2, q_norm_g, k_norm_g, gla_norm_g, w_out, norm2_g, w_up, w_down):
    w_in = w_in.astype(BF16)
    row = lambda a: a.reshape(DEPTH, 1, a.shape[-1])
    return {
        'g1': row(norm1_g),
        'win': w_in,
        'wsb': w_in[:, :, SB_COL0:],
        'wa2': jnp.pad(w_a2, ((0, 0), (0, LANES - GATE_RANK), (0, 0))).astype(BF16),
        'ba2': row(b_a2),
        'qg': row(jnp.tile(q_norm_g, (1, SB_HEADS))),
        'kg': row(jnp.tile(k_norm_g, (1, SB_HEADS))),
        'gg': row(gla_norm_g),
        'wo': w_out.astype(BF16),
        'g2': row(norm2_g),
        'wup': w_up.astype(BF16),
        'wdn': w_down.astype(BF16),
    }


def _stream_layer(x, pw, layer, kv_bufs, past_k, past_v, s0, tm, tf, tg, tq, n_sub):
    bsz, t, _ = x.shape
    x2d = x.reshape(bsz * t, D_MODEL)
    gq, gk, la, gv, gate, sq, skb, svb, skf, svf = _project(x2d, bsz, pw, layer, kv_bufs, tm)
    r3 = lambda a: a.reshape(bsz, t, a.shape[-1])
    o_gla, s_new = _gla(r3(gq), r3(gk), r3(la), r3(gv), s0, layer, tg)
    o_sb = _sb(r3(sq), r3(skb), r3(svb), past_k, past_v, layer, tq, n_sub)
    y = _finish(x2d, o_gla.reshape(bsz * t, GLA_WIDTH), gate, o_sb.reshape(bsz * t, SB_WIDTH), pw, layer, tf)
    return y.reshape(bsz, t, D_MODEL), (skf, svf), s_new


def kernel(x_prompt, x_sample, cache_sb_k, cache_sb_v, state_gla, norm1_g, w_in, w_a2, b_a2, q_norm_g, k_norm_g,
           gla_norm_g, w_out, norm2_g, w_up, w_down):
    pw = _prepare_weights(norm1_g, w_in, w_a2, b_a2, q_norm_g, k_norm_g, gla_norm_g, w_out, norm2_g, w_up, w_down)
    past_k = jnp.transpose(cache_sb_k, (0, 1, 3, 4, 2))
    past_v = jnp.transpose(cache_sb_v, (0, 1, 3, 4, 2))
    xp, xs = x_prompt, x_sample
    kv_p, kv_s, states_p, states_s = None, None, [], []
    for layer in range(DEPTH):
        xp, kv_p, sp = _stream_layer(xp, pw, layer, kv_p, None, None, None, tm=512, tf=512, tg=1024, tq=128, n_sub=16)
        xs, kv_s, ss = _stream_layer(xs, pw, layer, kv_s, past_k, past_v, state_gla, tm=512, tf=256, tg=CHUNK,
                                     tq=CHUNK, n_sub=1)
        states_p.append(sp)
        states_s.append(ss)
    return (xp, xs, kv_p[0], kv_p[1], jnp.stack(states_p), kv_s[0], kv_s[1], jnp.stack(states_s))
```
